```python
import jax, jax.numpy as jnp
from jax import lax
import numpy as np


D_MODEL = 1024
BATCH = 16
SEQ = 2048
DEPTH = 2

N_META = 16
NORM_EPS = 1e-6
N_EVEN = (DEPTH + 1) // 2
N_ODD = DEPTH // 2

GDN_HEADS = 4
GDN_HEAD_DIM = 128
GDN_WIDTH = GDN_HEADS * GDN_HEAD_DIM
GDN_CONV = 4
GDN_CHUNK = 64

CONF_WIDTH = D_MODEL - GDN_WIDTH
CONF_CONV = 31

SWA_HEAD_DIM = 64
SWA_Q_HEADS = D_MODEL // SWA_HEAD_DIM
SWA_KV_HEADS = 2
SWA_WINDOW = 128
SWA_BLOCK = SWA_WINDOW
SWA_Q_WIDTH = SWA_Q_HEADS * SWA_HEAD_DIM
SWA_KV_WIDTH = SWA_KV_HEADS * SWA_HEAD_DIM

EVEN_SPLITS = [3 * GDN_WIDTH, GDN_WIDTH, GDN_HEADS, GDN_HEADS, CONF_WIDTH, CONF_WIDTH, CONF_WIDTH]
ODD_SPLITS = [SWA_Q_WIDTH, SWA_KV_WIDTH, SWA_KV_WIDTH, SWA_Q_WIDTH]
EVEN_IN = sum(EVEN_SPLITS)
ODD_IN = sum(ODD_SPLITS)

kernel_name = 'hybrid_gdn_conformer_swa_sink'


def rms_norm(x, w):
    xf = x.astype(jnp.float32)
    y = xf * lax.rsqrt(jnp.mean(xf * xf, axis=-1, keepdims=True) + NORM_EPS)
    return (y * w.astype(jnp.float32)).astype(x.dtype)


def layer_norm(x, w, b):
    xf = x.astype(jnp.float32)
    mu = jnp.mean(xf, axis=-1, keepdims=True)
    var = jnp.mean(jnp.square(xf - mu), axis=-1, keepdims=True)
    y = (xf - mu) * lax.rsqrt(var + NORM_EPS) * w.astype(jnp.float32) + b.astype(jnp.float32)
    return y.astype(x.dtype)


def l2_normalize(x):
    return x * lax.rsqrt(jnp.sum(x * x, axis=-1, keepdims=True) + NORM_EPS)


def split_cols(a, sizes):
    return jnp.split(a, np.cumsum(sizes)[:-1].tolist(), axis=-1)


def causal_depthwise_conv(x, w):
    width = w.shape[0]
    return lax.conv_general_dilated(
        x, w.astype(x.dtype)[:, None, :], window_strides=(1,), padding=[(width - 1, 0)],
        dimension_numbers=('NWC', 'WIO', 'NWC'), feature_group_count=x.shape[-1])


def chunked_gated_delta_rule(q, k, v, g, beta):
    b, l, h, dk = q.shape
    dv = v.shape[-1]
    c = GDN_CHUNK
    pad = (-l) % c
    n = (l + pad) // c

    def to_chunks(a):
        a = jnp.pad(a, [(0, 0), (pad, 0)] + [(0, 0)] * (a.ndim - 2))
        a = a.reshape((b, n, c) + a.shape[2:])
        return jnp.moveaxis(a, 3, 1)

    q, k, v, g, beta = (to_chunks(a) for a in (q, k, v, g, beta))
    gc = jnp.cumsum(g, axis=-1)
    idx = jnp.arange(c)
    incl = idx[:, None] >= idx[None, :]
    strict = idx[:, None] > idx[None, :]
    decay_incl = jnp.exp(jnp.where(incl, gc[..., :, None] - gc[..., None, :], -jnp.inf))
    decay_strict = jnp.where(strict, decay_incl, 0.0)
    kb = k * beta[..., None]
    vb = v * beta[..., None]
    a_mat = jnp.einsum('bhncd,bhnsd->bhncs', kb, k) * decay_strict
    eye = jnp.eye(c, dtype=jnp.float32)
    t_mat = lax.linalg.triangular_solve(eye + a_mat, jnp.broadcast_to(eye, a_mat.shape),
                                        left_side=True, lower=True, unit_diagonal=True)
    u = jnp.einsum('bhncs,bhnse->bhnce', t_mat, vb)
    w = jnp.einsum('bhncs,bhnsd->bhncd', t_mat, kb * jnp.exp(gc)[..., None])
    qk = jnp.einsum('bhncd,bhnsd->bhncs', q, k) * decay_incl
    q_dec = q * jnp.exp(gc)[..., None]
    k_dec = k * jnp.exp(gc[..., -1:] - gc)[..., None]
    state_decay = jnp.exp(gc[..., -1])
    xs = tuple(jnp.moveaxis(a, 2, 0) for a in (u, w, q_dec, qk, k_dec, state_decay))

    def step(state, inp):
        u_c, w_c, qd_c, qk_c, kd_c, sd_c = inp
        v_new = u_c - jnp.einsum('bhcd,bhde->bhce', w_c, state)
        out = (jnp.einsum('bhcd,bhde->bhce', qd_c, state)
               + jnp.einsum('bhcs,bhse->bhce', qk_c, v_new))
        state = state * sd_c[..., None, None] + jnp.einsum('bhcd,bhce->bhde', kd_c, v_new)
        return state, out

    s0 = jnp.zeros((b, h, dk, dv), jnp.float32)
    _, out = lax.scan(step, s0, xs)
    out = jnp.transpose(out, (1, 0, 3, 2, 4)).reshape(b, n * c, h, dv)
    return out[:, pad:]


def sliding_window_sink_attention(q, k, v, sinks):
    bsz, l, _ = q.shape
    blk = SWA_BLOCK
    pad = (-l) % blk
    nb = (l + pad) // blk
    grp = SWA_Q_HEADS // SWA_KV_HEADS
    k_meta = k[:, :N_META].reshape(bsz, N_META, SWA_KV_HEADS, SWA_HEAD_DIM)
    v_meta = v[:, :N_META].reshape(bsz, N_META, SWA_KV_HEADS, SWA_HEAD_DIM)

    def blocks(a, heads_shape):
        a = jnp.pad(a, ((0, 0), (pad, 0), (0, 0)))
        return a.reshape((bsz, nb, blk) + heads_shape)

    def band(a):
        prev = jnp.concatenate([jnp.zeros_like(a[:, :1]), a[:, :-1]], axis=1)
        return jnp.concatenate([prev, a], axis=2)

    qb = blocks(q, (SWA_KV_HEADS, grp, SWA_HEAD_DIM)) * (SWA_HEAD_DIM ** -0.5)
    kb = band(blocks(k, (SWA_KV_HEADS, SWA_HEAD_DIM)))
    vb = band(blocks(v, (SWA_KV_HEADS, SWA_HEAD_DIM)))
    s_meta = jnp.einsum('bnqhgd,bmhd->bhgnqm', qb, k_meta)
    s_band = jnp.einsum('bnqhgd,bnkhd->bhgnqk', qb, kb)
    scores = jnp.concatenate([s_meta, s_band], axis=-1).astype(jnp.float32)

    pos_q = jnp.arange(nb * blk).reshape(nb, blk) - pad
    pos_k = jnp.concatenate([pos_q - blk, pos_q], axis=1)
    rel = pos_q[:, :, None] - pos_k[:, None, :]
    band_mask = (pos_k[:, None, :] >= N_META) & (rel >= 0) & (rel < SWA_WINDOW)
    meta_mask = jnp.arange(N_META)[None, None, :] <= pos_q[:, :, None]
    mask = jnp.concatenate([meta_mask, band_mask], axis=-1)
    scores = jnp.where(mask, scores, -jnp.inf)

    sink = sinks.astype(jnp.float32).reshape(SWA_KV_HEADS, grp)[None, :, :, None, None, None]
    m = jnp.maximum(scores.max(axis=-1, keepdims=True), sink)
    p = jnp.exp(scores - m)
    denom = p.sum(axis=-1) + jnp.exp(sink - m)[..., 0]
    p = p / denom[..., None]
    o = (jnp.einsum('bhgnqm,bmhd->bnqhgd', p[..., :N_META], v_meta.astype(jnp.float32))
         + jnp.einsum('bhgnqk,bnkhd->bnqhgd', p[..., N_META:], vb.astype(jnp.float32)))
    return o.reshape(bsz, nb * blk, SWA_Q_WIDTH)[:, pad:].astype(q.dtype)


def even_layer(x, pre_norm, w_in, qkv_conv, a_log, dt_bias, out_norm,
               dw_conv, dw_bias, ln_w, ln_b, w_out, post_norm):
    bsz, l, _ = x.shape
    h = rms_norm(x, pre_norm)
    proj = jnp.einsum('bld,de->ble', h, w_in)
    qkv, z_a, b_a, a_a, glu_v, glu_g, z_b = split_cols(proj, EVEN_SPLITS)

    qkv = jax.nn.silu(causal_depthwise_conv(qkv, qkv_conv)).astype(jnp.float32)
    qkv = qkv.reshape(bsz, l, 3, GDN_HEADS, GDN_HEAD_DIM)
    q = l2_normalize(qkv[:, :, 0]) * (GDN_HEAD_DIM ** -0.5)
    k = l2_normalize(qkv[:, :, 1])
    v = qkv[:, :, 2]
    beta = jax.nn.sigmoid(b_a.astype(jnp.float32))
    g = -jnp.exp(a_log.astype(jnp.float32)) * jax.nn.softplus(
        a_a.astype(jnp.float32) + dt_bias.astype(jnp.float32))
    o_a = chunked_gated_delta_rule(q, k, v, g, beta)
    o_a = rms_norm(o_a, out_norm).reshape(bsz, l, GDN_WIDTH).astype(x.dtype) * jax.nn.silu(z_a)

    c = glu_v * jax.nn.sigmoid(glu_g)
    c = causal_depthwise_conv(c, dw_conv) + dw_bias
    c = jax.nn.silu(layer_norm(c, ln_w, ln_b))
    o_b = c * jax.nn.silu(z_b)

    y = jnp.einsum('ble,ed->bld', jnp.concatenate([o_a, o_b], axis=-1), w_out)
    return x + rms_norm(y, post_norm)


def odd_layer(x, pre_norm, w_in, sinks, w_out, post_norm):
    h = rms_norm(x, pre_norm)
    proj = jnp.einsum('bld,de->ble', h, w_in)
    q, k, v, z = split_cols(proj, ODD_SPLITS)
    o = sliding_window_sink_attention(q, k, v, sinks) * jax.nn.silu(z)
    y = jnp.einsum('ble,ed->bld', o, w_out)
    return x + rms_norm(y, post_norm)


def setup_inputs(seed: int = 0) -> dict:
    key = jax.random.key(seed)
    ks = jax.random.split(key, 20)
    f32 = jnp.float32
    d = D_MODEL

    def nrm(k, shape, scale):
        return jax.random.normal(k, shape, f32) * scale

    def gain(k, shape):
        return 1.0 + 0.02 * jax.random.normal(k, shape, f32)

    dt = jnp.exp(jax.random.uniform(ks[5], (N_EVEN, GDN_HEADS), f32, np.log(1e-3), np.log(1e-1)))
    return {
        'x': nrm(ks[0], (BATCH, SEQ, d), 1.0),
        'meta_tokens': nrm(ks[1], (N_META, d), 1.0),
        'even_pre_norm': gain(ks[2], (N_EVEN, d)),
        'even_w_in': nrm(ks[3], (N_EVEN, d, EVEN_IN), d ** -0.5),
        'even_qkv_conv': nrm(ks[4], (N_EVEN, GDN_CONV, 3 * GDN_WIDTH), GDN_CONV ** -0.5),
        'even_a_log': jnp.log(jax.random.uniform(ks[6], (N_EVEN, GDN_HEADS), f32, 1.0, 16.0)),
        'even_dt_bias': dt + jnp.log(-jnp.expm1(-dt)),
        'even_out_norm': gain(ks[7], (N_EVEN, GDN_HEAD_DIM)),
        'even_dw_conv': nrm(ks[8], (N_EVEN, CONF_CONV, CONF_WIDTH), CONF_CONV ** -0.5),
        'even_dw_bias': nrm(ks[9], (N_EVEN, CONF_WIDTH), 0.02),
        'even_ln_w': gain(ks[10], (N_EVEN, CONF_WIDTH)),
        'even_ln_b': nrm(ks[11], (N_EVEN, CONF_WIDTH), 0.02),
        'even_w_out': nrm(ks[12], (N_EVEN, GDN_WIDTH + CONF_WIDTH, d), (GDN_WIDTH + CONF_WIDTH) ** -0.5),
        'even_post_norm': gain(ks[13], (N_EVEN, d)),
        'odd_pre_norm': gain(ks[14], (N_ODD, d)),
        'odd_w_in': nrm(ks[15], (N_ODD, d, ODD_IN), d ** -0.5),
        'odd_sinks': nrm(ks[16], (N_ODD, SWA_Q_HEADS), 0.5),
        'odd_w_out': nrm(ks[17], (N_ODD, SWA_Q_WIDTH, d), SWA_Q_WIDTH ** -0.5),
        'odd_post_norm': gain(ks[18], (N_ODD, d)),
    }


def reference(x, meta_tokens, even_pre_norm, even_w_in, even_qkv_conv, even_a_log,
              even_dt_bias, even_out_norm, even_dw_conv, even_dw_bias, even_ln_w, even_ln_b,
              even_w_out, even_post_norm, odd_pre_norm, odd_w_in, odd_sinks, odd_w_out,
              odd_post_norm):
    bsz = x.shape[0]
    meta = jnp.broadcast_to(meta_tokens.astype(x.dtype)[None], (bsz, N_META, D_MODEL))
    h = jnp.concatenate([meta, x], axis=1)
    for layer in range(DEPTH):
        i = layer // 2
        if layer % 2 == 0:
            h = even_layer(h, even_pre_norm[i], even_w_in[i], even_qkv_conv[i], even_a_log[i],
                           even_dt_bias[i], even_out_norm[i], even_dw_conv[i], even_dw_bias[i],
                           even_ln_w[i], even_ln_b[i], even_w_out[i], even_post_norm[i])
        else:
            h = odd_layer(h, odd_pre_norm[i], odd_w_in[i], odd_sinks[i], odd_w_out[i],
                          odd_post_norm[i])
    return h[:, N_META:]
```

```python
import functools

import jax
import jax.numpy as jnp
from jax import lax
from jax.experimental import pallas as pl
from jax.experimental.pallas import tpu as pltpu

D_MODEL = 1024
N_META = 16
NORM_EPS = 1e-6

GDN_HEADS = 4
GDN_HEAD_DIM = 128
GDN_WIDTH = GDN_HEADS * GDN_HEAD_DIM
GDN_CONV = 4
GDN_CHUNK = 64
CONF_WIDTH = D_MODEL - GDN_WIDTH
CONF_CONV = 31

SWA_HEAD_DIM = 64
SWA_Q_HEADS = D_MODEL // SWA_HEAD_DIM
SWA_KV_HEADS = 2
SWA_WINDOW = 128
SWA_Q_WIDTH = SWA_Q_HEADS * SWA_HEAD_DIM
SWA_KV_WIDTH = SWA_KV_HEADS * SWA_HEAD_DIM

LANES = 128
ROWS = 128
PAD_ROWS = ROWS - N_META

C_QKV = 0
C_ZA = C_QKV + 3 * GDN_WIDTH
C_GV = C_ZA + GDN_WIDTH
C_GG = C_GV + CONF_WIDTH
C_ZB = C_GG + CONF_WIDTH
C_BA = C_ZB + CONF_WIDTH
EVEN_COLS = C_BA + LANES

QKV_HALO = 8
CONF_HALO = 32

VMEM_LIMIT_BYTES = 56 * 1024 * 1024

_NT = (((1,), (1,)), ((), ()))


def _dot(a, b):
    return jnp.dot(a.astype(jnp.bfloat16), b.astype(jnp.bfloat16), preferred_element_type=jnp.float32)


def _dot_nt(a, b):
    return lax.dot_general(a.astype(jnp.bfloat16), b.astype(jnp.bfloat16), _NT,
                           preferred_element_type=jnp.float32)


def _rms(x, w):
    return x * lax.rsqrt(jnp.mean(x * x, axis=-1, keepdims=True) + NORM_EPS) * w


def _sigmoid(x):
    return 1.0 / (1.0 + jnp.exp(-x))


def _silu(x):
    return x * _sigmoid(x)


def _softplus(x):
    return jnp.maximum(x, 0.0) + jnp.log(1.0 + jnp.exp(-jnp.abs(x)))


def _l2n(x):
    return x * lax.rsqrt(jnp.sum(x * x, axis=-1, keepdims=True) + NORM_EPS)


def _gdn_head(q, k, v, beta, gc, glast, sd, state):
    n = ROWS
    row = lax.broadcasted_iota(jnp.int32, (n, n), 0)
    col = lax.broadcasted_iota(jnp.int32, (n, n), 1)
    same = (row >= GDN_CHUNK) == (col >= GDN_CHUNK)
    incl = same & (col <= row)
    strict = same & (col < row)
    eye = row == col

    gc_b = jnp.broadcast_to(gc, (n, n))
    gc_row = jnp.sum(jnp.where(eye, gc_b, 0.0), axis=0, keepdims=True)
    decay = jnp.exp(jnp.where(incl, gc_b - gc_row, 0.0))
    dec_incl = jnp.where(incl, decay, 0.0)
    dec_strict = jnp.where(strict, decay, 0.0)

    kb = k * beta
    vb = v * beta
    kq = _dot_nt(jnp.concatenate([kb, q], axis=0), k)
    a = kq[:n] * dec_strict
    qk = kq[n:] * dec_incl

    p = jnp.where(eye, 1.0, 0.0) - a
    bm = _dot(a, a)
    levels = GDN_CHUNK.bit_length() - 1
    for lvl in range(1, levels):
        if lvl < levels - 1:
            r = _dot(bm, jnp.concatenate([bm, p], axis=1))
            bm = r[:, :n]
            p = p + r[:, n:]
        else:
            p = p + _dot(bm, p)

    egc = jnp.exp(gc)
    uw = _dot(p, jnp.concatenate([vb, kb * egc], axis=1))
    u = uw[:, :GDN_HEAD_DIM]
    w = uw[:, GDN_HEAD_DIM:]
    qd = q * egc
    kd_t = jnp.transpose(k * jnp.exp(glast - gc))

    outs = []
    zeros = jnp.zeros((GDN_CHUNK, GDN_HEAD_DIM), jnp.float32)
    for c in range(n // GDN_CHUNK):
        sl = slice(c * GDN_CHUNK, (c + 1) * GDN_CHUNK)
        r1 = _dot(jnp.concatenate([w[sl], qd[sl]], axis=0), state)
        v_new = u[sl] - r1[:GDN_CHUNK]
        v_full = jnp.concatenate([v_new, zeros] if c == 0 else [zeros, v_new], axis=0)
        r2 = _dot(jnp.concatenate([qk[sl], kd_t], axis=0), v_full)
        outs.append(r1[GDN_CHUNK:] + r2[:GDN_CHUNK])
        state = state * sd[c] + r2[GDN_CHUNK:]
    return jnp.concatenate(outs, axis=0), state


def _even_kernel(x_ref, meta_ref, pre_ref, win_ref, qkvw_ref, alog_ref, dtb_ref, onorm_ref,
                 dww_ref, dwb_ref, lnw_ref, lnb_ref, wout_ref, post_ref, out_ref,
                 qkv_ext, c_ext, s_ref):
    j = pl.program_id(1)

    @pl.when(j == 0)
    def _():
        qkv_ext[0:QKV_HALO, :] = jnp.zeros((QKV_HALO, 3 * GDN_WIDTH), jnp.float32)
        c_ext[0:CONF_HALO, :] = jnp.zeros((CONF_HALO, CONF_WIDTH), jnp.float32)
        s_ref[...] = jnp.zeros(s_ref.shape, jnp.float32)

    h = jnp.where(j == 0, meta_ref[...], x_ref[0])
    hn = _rms(h, pre_ref[...]).astype(jnp.bfloat16)

    def proj(c0, width):
        return jnp.dot(hn, win_ref[:, c0:c0 + width], preferred_element_type=jnp.float32)

    qkv_ext[QKV_HALO:QKV_HALO + ROWS, :] = proj(C_QKV, 3 * GDN_WIDTH)
    conv = qkvw_ref[0:1, :] * qkv_ext[QKV_HALO - 3:QKV_HALO - 3 + ROWS, :]
    for t in range(1, GDN_CONV):
        conv = conv + qkvw_ref[t:t + 1, :] * qkv_ext[QKV_HALO - 3 + t:QKV_HALO - 3 + t + ROWS, :]
    qkv = _silu(conv)
    qkv_ext[0:QKV_HALO, :] = qkv_ext[ROWS:ROWS + QKV_HALO, :]

    gate = proj(C_BA, LANES)
    rowi = lax.broadcasted_iota(jnp.int32, (ROWS, LANES), 0)
    real = (j * ROWS + rowi) >= PAD_ROWS
    beta_all = _sigmoid(gate)
    g_all = jnp.where(real, -jnp.exp(alog_ref[...]) * _softplus(gate + dtb_ref[...]), 0.0)
    r2 = lax.broadcasted_iota(jnp.int32, (ROWS, ROWS), 0)
    c2 = lax.broadcasted_iota(jnp.int32, (ROWS, ROWS), 1)
    tri = jnp.where(((r2 >= GDN_CHUNK) == (c2 >= GDN_CHUNK)) & (c2 <= r2), 1.0, 0.0)
    gc_all = jnp.dot(tri, g_all, preferred_element_type=jnp.float32,
                     precision=lax.Precision.HIGHEST)

    za = proj(C_ZA, GDN_WIDTH)
    o_a = []
    top = rowi[:, 0:1] < GDN_CHUNK
    for hd in range(GDN_HEADS):
        sl = slice(hd * GDN_HEAD_DIM, (hd + 1) * GDN_HEAD_DIM)
        q = _l2n(qkv[:, sl]) * (GDN_HEAD_DIM ** -0.5)
        k = _l2n(qkv[:, GDN_WIDTH + hd * GDN_HEAD_DIM:GDN_WIDTH + (hd + 1) * GDN_HEAD_DIM])
        v = qkv[:, 2 * GDN_WIDTH + hd * GDN_HEAD_DIM:2 * GDN_WIDTH + (hd + 1) * GDN_HEAD_DIM]
        beta = beta_all[:, hd:hd + 1]
        gc = gc_all[:, GDN_HEADS + hd:GDN_HEADS + hd + 1]
        gl0 = gc[GDN_CHUNK - 1:GDN_CHUNK, :]
        gl1 = gc[ROWS - 1:ROWS, :]
        glast = jnp.where(top, gl0, gl1)
        o, s_new = _gdn_head(q, k, v, beta, gc, glast, (jnp.exp(gl0), jnp.exp(gl1)), s_ref[hd])
        s_ref[hd] = s_new
        o_a.append(_rms(o, onorm_ref[...]) * _silu(za[:, sl]))

    c_ext[CONF_HALO:CONF_HALO + ROWS, :] = proj(C_GV, CONF_WIDTH) * _sigmoid(proj(C_GG, CONF_WIDTH))
    off = CONF_HALO - (CONF_CONV - 1)
    acc = dwb_ref[...] + dww_ref[0:1, :] * c_ext[off:off + ROWS, :]
    for t in range(1, CONF_CONV):
        acc = acc + dww_ref[t:t + 1, :] * c_ext[off + t:off + t + ROWS, :]
    c_ext[0:CONF_HALO, :] = c_ext[ROWS:ROWS + CONF_HALO, :]
    mu = jnp.mean(acc, axis=-1, keepdims=True)
    cen = acc - mu
    var = jnp.mean(cen * cen, axis=-1, keepdims=True)
    ln = cen * lax.rsqrt(var + NORM_EPS) * lnw_ref[...] + lnb_ref[...]
    o_b = _silu(ln) * _silu(proj(C_ZB, CONF_WIDTH))

    mixed = jnp.concatenate(o_a + [o_b], axis=1).astype(jnp.bfloat16)
    y = jnp.dot(mixed, wout_ref[...], preferred_element_type=jnp.float32)
    out_ref[0] = h + _rms(y, post_ref[...])


def _odd_kernel(sinks_ref, h_ref, pre_ref, win_ref, wout_ref, post_ref, out_ref, k2_ref, v2_ref):
    j = pl.program_id(1)
    nk = N_META + 2 * ROWS
    prev0, cur0 = N_META, N_META + ROWS

    h = h_ref[0]
    hn = _rms(h, pre_ref[...]).astype(jnp.bfloat16)

    def proj(c0, width):
        return jnp.dot(hn, win_ref[:, c0:c0 + width], preferred_element_type=jnp.float32)

    kv = proj(SWA_Q_WIDTH, 2 * SWA_KV_WIDTH)
    lane = lax.broadcasted_iota(jnp.int32, (ROWS, LANES), 1)
    lo = lane < SWA_HEAD_DIM
    kk, vv = kv[:, :LANES], kv[:, LANES:]
    kr = pltpu.roll(kk, SWA_HEAD_DIM, axis=1)
    vr = pltpu.roll(vv, SWA_HEAD_DIM, axis=1)
    k2 = (jnp.where(lo, kk, kr), jnp.where(lo, kr, kk))
    v2 = (jnp.where(lo, vv, vr), jnp.where(lo, vr, vv))

    @pl.when(j == 0)
    def _():
        for g in range(SWA_KV_HEADS):
            k2_ref[g, 0:N_META, :] = k2[g][PAD_ROWS:, :].astype(jnp.bfloat16)
            v2_ref[g, 0:N_META, :] = v2[g][PAD_ROWS:, :].astype(jnp.bfloat16)
            k2_ref[g, prev0:cur0, :] = jnp.zeros((ROWS, LANES), jnp.bfloat16)
            v2_ref[g, prev0:cur0, :] = jnp.zeros((ROWS, LANES), jnp.bfloat16)

    for g in range(SWA_KV_HEADS):
        k2_ref[g, cur0:nk, :] = k2[g].astype(jnp.bfloat16)
        v2_ref[g, cur0:nk, :] = v2[g].astype(jnp.bfloat16)

    q = proj(0, SWA_Q_WIDTH) * (SWA_HEAD_DIM ** -0.5)
    z = proj(SWA_Q_WIDTH + 2 * SWA_KV_WIDTH, SWA_Q_WIDTH)

    r = lax.broadcasted_iota(jnp.int32, (2 * ROWS, nk), 0) & (ROWS - 1)
    c = lax.broadcasted_iota(jnp.int32, (2 * ROWS, nk), 1)
    valid2 = (((c < prev0) & ((j >= 1) | (c <= r - PAD_ROWS)))
              | ((c >= prev0) & (c < cur0) & (c - prev0 > r) & (j >= 2))
              | ((c >= cur0) & (c - cur0 <= r) & (j >= 1)))
    first = lax.broadcasted_iota(jnp.int32, (2 * ROWS, 1), 0) < ROWS

    pairs = SWA_Q_HEADS // 2
    o = []
    for p in range(pairs):
        g = (2 * p) // (SWA_Q_HEADS // SWA_KV_HEADS)
        qp = q[:, p * LANES:(p + 1) * LANES]
        lhs = jnp.concatenate([jnp.where(lo, qp, 0.0), jnp.where(lo, 0.0, qp)], axis=0)
        s = lax.dot_general(lhs.astype(jnp.bfloat16), k2_ref[g], _NT,
                            preferred_element_type=jnp.float32)
        s = jnp.where(valid2, s, -jnp.inf)
        sink = jnp.where(first, sinks_ref[2 * p], sinks_ref[2 * p + 1])
        m = jnp.maximum(jnp.max(s, axis=-1, keepdims=True), sink)
        e = jnp.exp(s - m)
        den = jnp.sum(e, axis=-1, keepdims=True) + jnp.exp(sink - m)
        o2 = jnp.dot((e / den).astype(jnp.bfloat16), v2_ref[g], preferred_element_type=jnp.float32)
        o.append(jnp.where(lo, o2[:ROWS], o2[ROWS:]))

    for g in range(SWA_KV_HEADS):
        k2_ref[g, prev0:cur0, :] = k2_ref[g, cur0:nk, :]
        v2_ref[g, prev0:cur0, :] = v2_ref[g, cur0:nk, :]

    gated = (jnp.concatenate(o, axis=1) * _silu(z)).astype(jnp.bfloat16)
    y = jnp.dot(gated, wout_ref[...], preferred_element_type=jnp.float32)
    out_ref[0] = h + _rms(y, post_ref[...])


def _const_spec(shape):
    return pl.BlockSpec(shape, lambda b, j: (0,) * len(shape))


def _lane_pad(v, offset):
    return jnp.zeros((1, LANES), jnp.float32).at[0, offset:offset + v.shape[0]].set(v)


def kernel(x, meta_tokens, even_pre_norm, even_w_in, even_qkv_conv, even_a_log, even_dt_bias,
           even_out_norm, even_dw_conv, even_dw_bias, even_ln_w, even_ln_b, even_w_out,
           even_post_norm, odd_pre_norm, odd_w_in, odd_sinks, odd_w_out, odd_post_norm):
    bsz, seq, d = x.shape
    assert d == D_MODEL and seq % ROWS == 0
    nblk = seq // ROWS + 1
    f32 = jnp.float32

    w0 = even_w_in[0]
    s_qkv, s_za, s_b, s_a, s_gv, s_gg, s_zb = jnp.split(
        w0, [1536, 2048, 2052, 2056, 2568, 3080], axis=1)
    gate_cols = jnp.concatenate(
        [s_b, s_a, jnp.zeros((d, LANES - 2 * GDN_HEADS), f32)], axis=1)
    w_in0 = jnp.concatenate([s_qkv, s_za, s_gv, s_gg, s_zb, gate_cols], axis=1).astype(jnp.bfloat16)
    w_out0 = even_w_out[0].astype(jnp.bfloat16)
    w_in1 = odd_w_in[0].astype(jnp.bfloat16)
    w_out1 = odd_w_out[0].astype(jnp.bfloat16)
    meta_blk = jnp.concatenate([jnp.zeros((PAD_ROWS, d), f32), meta_tokens.astype(f32)], axis=0)
    alog = _lane_pad(even_a_log[0], GDN_HEADS)
    dtb = _lane_pad(even_dt_bias[0], GDN_HEADS)
    row = lambda v: v.reshape(1, -1).astype(f32)

    params = pltpu.CompilerParams(dimension_semantics=("parallel", "arbitrary"),
                                  vmem_limit_bytes=VMEM_LIMIT_BYTES)

    h1 = pl.pallas_call(
        _even_kernel,
        grid=(bsz, nblk),
        in_specs=[
            pl.BlockSpec((1, ROWS, d), lambda b, j: (b, jnp.maximum(j - 1, 0), 0)),
            _const_spec((ROWS, d)),
            _const_spec((1, d)),
            _const_spec((d, EVEN_COLS)),
            _const_spec((GDN_CONV, 3 * GDN_WIDTH)),
            _const_spec((1, LANES)),
            _const_spec((1, LANES)),
            _const_spec((1, GDN_HEAD_DIM)),
            _const_spec((CONF_CONV, CONF_WIDTH)),
            _const_spec((1, CONF_WIDTH)),
            _const_spec((1, CONF_WIDTH)),
            _const_spec((1, CONF_WIDTH)),
            _const_spec((d, d)),
            _const_spec((1, d)),
        ],
        out_specs=pl.BlockSpec((1, ROWS, d), lambda b, j: (b, j, 0)),
        out_shape=jax.ShapeDtypeStruct((bsz, nblk * ROWS, d), f32),
        scratch_shapes=[
            pltpu.VMEM((QKV_HALO + ROWS, 3 * GDN_WIDTH), f32),
            pltpu.VMEM((CONF_HALO + ROWS, CONF_WIDTH), f32),
            pltpu.VMEM((GDN_HEADS, GDN_HEAD_DIM, GDN_HEAD_DIM), f32),
        ],
        compiler_params=params,
        name="even_layer",
    )(x, meta_blk, row(even_pre_norm[0]), w_in0, even_qkv_conv[0].astype(f32), alog, dtb,
      row(even_out_norm[0]), even_dw_conv[0].astype(f32), row(even_dw_bias[0]),
      row(even_ln_w[0]), row(even_ln_b[0]), w_out0, row(even_post_norm[0]))

    nk = N_META + 2 * ROWS
    out = pl.pallas_call(
        _odd_kernel,
        grid_spec=pltpu.PrefetchScalarGridSpec(
            num_scalar_prefetch=1,
            grid=(bsz, nblk),
            in_specs=[
                pl.BlockSpec((1, ROWS, d), lambda b, j, s: (b, j, 0)),
                pl.BlockSpec((1, d), lambda b, j, s: (0, 0)),
                pl.BlockSpec((d, 2 * SWA_Q_WIDTH + 2 * SWA_KV_WIDTH), lambda b, j, s: (0, 0)),
                pl.BlockSpec((SWA_Q_WIDTH, d), lambda b, j, s: (0, 0)),
                pl.BlockSpec((1, d), lambda b, j, s: (0, 0)),
            ],
            out_specs=pl.BlockSpec((1, ROWS, d), lambda b, j, s: (b, jnp.maximum(j - 1, 0), 0)),
            scratch_shapes=[
                pltpu.VMEM((SWA_KV_HEADS, nk, LANES), jnp.bfloat16),
                pltpu.VMEM((SWA_KV_HEADS, nk, LANES), jnp.bfloat16),
            ],
        ),
        out_shape=jax.ShapeDtypeStruct((bsz, seq, d), f32),
        compiler_params=params,
        name="odd_layer",
    )(odd_sinks[0].astype(f32), h1, row(odd_pre_norm[0]), w_in1, w_out1, row(odd_post_norm[0]))
    return out
```

```python
import functools

import jax
import jax.numpy as jnp
from jax import lax
from jax.experimental import pallas as pl
from jax.experimental.pallas import tpu as pltpu

D_MODEL = 1024
N_META = 16
NORM_EPS = 1e-6

GDN_HEADS = 4
GDN_HEAD_DIM = 128
GDN_WIDTH = GDN_HEADS * GDN_HEAD_DIM
GDN_CONV = 4
GDN_CHUNK = 64
CONF_WIDTH = D_MODEL - GDN_WIDTH
CONF_CONV = 31

SWA_HEAD_DIM = 64
SWA_Q_HEADS = D_MODEL // SWA_HEAD_DIM
SWA_KV_HEADS = 2
SWA_WINDOW = 128
SWA_Q_WIDTH = SWA_Q_HEADS * SWA_HEAD_DIM
SWA_KV_WIDTH = SWA_KV_HEADS * SWA_HEAD_DIM

LANES = 128
SUBLANES = 8
ROWS = 128
PAD_ROWS = ROWS - N_META

C_QKV = 0
C_ZA = C_QKV + 3 * GDN_WIDTH
C_GV = C_ZA + GDN_WIDTH
C_GG = C_GV + CONF_WIDTH
C_ZB = C_GG + CONF_WIDTH
C_BA = C_ZB + CONF_WIDTH
EVEN_COLS = C_BA + LANES

QKV_HALO = 8
CONF_HALO = 32

VMEM_LIMIT_BYTES = 56 * 1024 * 1024

_NT = (((1,), (1,)), ((), ()))


def _dot(a, b):
    return jnp.dot(a.astype(jnp.bfloat16), b.astype(jnp.bfloat16), preferred_element_type=jnp.float32)


def _dot_nt(a, b):
    return lax.dot_general(a.astype(jnp.bfloat16), b.astype(jnp.bfloat16), _NT,
                           preferred_element_type=jnp.float32)


def _rms(x, w):
    return x * lax.rsqrt(jnp.mean(x * x, axis=-1, keepdims=True) + NORM_EPS) * w


def _sigmoid(x):
    return 1.0 / (1.0 + jnp.exp(-x))


def _silu(x):
    return x * _sigmoid(x)


def _softplus(x):
    return jnp.maximum(x, 0.0) + jnp.log(1.0 + jnp.exp(-jnp.abs(x)))


def _l2n(x):
    return x * lax.rsqrt(jnp.sum(x * x, axis=-1, keepdims=True) + NORM_EPS)


def _gdn_block(q, k, v, beta, gc, glast, sd, state):
    n = ROWS
    heads = range(len(q))
    row = lax.broadcasted_iota(jnp.int32, (n, n), 0)
    col = lax.broadcasted_iota(jnp.int32, (n, n), 1)
    same = (row >= GDN_CHUNK) == (col >= GDN_CHUNK)
    incl = same & (col <= row)
    strict = same & (col < row)
    eye = row == col
    ident = jnp.where(eye, 1.0, 0.0)

    kb = [k[h] * beta[h] for h in heads]
    kq = [_dot_nt(jnp.concatenate([kb[h], q[h]], axis=0), k[h]) for h in heads]

    gc_b = [jnp.broadcast_to(gc[h], (n, n)) for h in heads]
    gc_row = [jnp.sum(jnp.where(eye, gc_b[h], 0.0), axis=0, keepdims=True) for h in heads]
    decay = [jnp.exp(jnp.where(incl, gc_b[h] - gc_row[h], 0.0)) for h in heads]
    a = [kq[h][:n] * jnp.where(strict, decay[h], 0.0) for h in heads]
    qk = [kq[h][n:] * jnp.where(incl, decay[h], 0.0) for h in heads]

    p = [ident - a[h] for h in heads]
    bm = [_dot(a[h], a[h]) for h in heads]
    levels = GDN_CHUNK.bit_length() - 1
    for lvl in range(1, levels):
        if lvl < levels - 1:
            r = [_dot(bm[h], jnp.concatenate([bm[h], p[h]], axis=1)) for h in heads]
            bm = [r[h][:, :n] for h in heads]
            p = [p[h] + r[h][:, n:] for h in heads]
        else:
            p = [p[h] + _dot(bm[h], p[h]) for h in heads]

    egc = [jnp.exp(gc[h]) for h in heads]
    uw = [_dot(p[h], jnp.concatenate([v[h] * beta[h], kb[h] * egc[h]], axis=1)) for h in heads]
    qd = [q[h] * egc[h] for h in heads]
    kd_t = [jnp.transpose(k[h] * jnp.exp(glast[h] - gc[h])) for h in heads]

    outs = [[] for _ in heads]
    state = list(state)
    zeros = jnp.zeros((GDN_CHUNK, GDN_HEAD_DIM), jnp.float32)
    for c in range(n // GDN_CHUNK):
        sl = slice(c * GDN_CHUNK, (c + 1) * GDN_CHUNK)
        r1 = [_dot(jnp.concatenate([uw[h][sl, GDN_HEAD_DIM:], qd[h][sl]], axis=0), state[h])
              for h in heads]
        v_new = [uw[h][sl, :GDN_HEAD_DIM] - r1[h][:GDN_CHUNK] for h in heads]
        v_full = [jnp.concatenate([v_new[h], zeros] if c == 0 else [zeros, v_new[h]], axis=0)
                  for h in heads]
        r2 = [_dot(jnp.concatenate([qk[h][sl], kd_t[h]], axis=0), v_full[h]) for h in heads]
        for h in heads:
            outs[h].append(r1[h][GDN_CHUNK:] + r2[h][:GDN_CHUNK])
            state[h] = state[h] * sd[h][c] + r2[h][GDN_CHUNK:]
    return [jnp.concatenate(o, axis=0) for o in outs], state


def _causal_conv(ext, w_ref, taps, halo, acc=None):
    n = ext.shape[0]
    first = halo - (taps - 1)
    for phase in range(SUBLANES):
        starts = [first + t for t in range(taps) if (first + t) % SUBLANES == phase]
        if not starts:
            continue
        shifted = ext if phase == 0 else pltpu.roll(ext, n - phase, axis=0)
        for st in starts:
            t = st - first
            term = w_ref[t:t + 1, :] * shifted[st - phase:st - phase + ROWS, :]
            acc = term if acc is None else acc + term
    return acc


def _even_kernel(x_ref, meta_ref, pre_ref, win_ref, qkvw_ref, alog_ref, dtb_ref, onorm_ref,
                 dww_ref, dwb_ref, lnw_ref, lnb_ref, wout_ref, post_ref, out_ref,
                 qkv_ext, c_ext, s_ref):
    j = pl.program_id(1)

    @pl.when(j == 0)
    def _():
        qkv_ext[0:QKV_HALO, :] = jnp.zeros((QKV_HALO, 3 * GDN_WIDTH), jnp.float32)
        c_ext[0:CONF_HALO, :] = jnp.zeros((CONF_HALO, CONF_WIDTH), jnp.float32)
        s_ref[...] = jnp.zeros(s_ref.shape, jnp.float32)

    h = jnp.where(j == 0, meta_ref[...], x_ref[0])
    hn = _rms(h, pre_ref[...]).astype(jnp.bfloat16)

    def proj(c0, width):
        return jnp.dot(hn, win_ref[:, c0:c0 + width], preferred_element_type=jnp.float32)

    qkv_ext[QKV_HALO:QKV_HALO + ROWS, :] = proj(C_QKV, 3 * GDN_WIDTH)
    qkv = _silu(_causal_conv(qkv_ext[...], qkvw_ref, GDN_CONV, QKV_HALO))
    qkv_ext[0:QKV_HALO, :] = qkv_ext[ROWS:ROWS + QKV_HALO, :]

    gate = proj(C_BA, LANES)
    rowi = lax.broadcasted_iota(jnp.int32, (ROWS, LANES), 0)
    real = (j * ROWS + rowi) >= PAD_ROWS
    beta_all = _sigmoid(gate)
    g_all = jnp.where(real, -jnp.exp(alog_ref[...]) * _softplus(gate + dtb_ref[...]), 0.0)
    r2 = lax.broadcasted_iota(jnp.int32, (ROWS, ROWS), 0)
    c2 = lax.broadcasted_iota(jnp.int32, (ROWS, ROWS), 1)
    tri = jnp.where(((r2 >= GDN_CHUNK) == (c2 >= GDN_CHUNK)) & (c2 <= r2), 1.0, 0.0)
    gc_all = jnp.dot(tri, g_all, preferred_element_type=jnp.float32,
                     precision=lax.Precision.HIGHEST)

    top = rowi[:, 0:1] < GDN_CHUNK
    hs = range(GDN_HEADS)
    hsl = [slice(hd * GDN_HEAD_DIM, (hd + 1) * GDN_HEAD_DIM) for hd in hs]
    q = [_l2n(qkv[:, hsl[hd]]) * (GDN_HEAD_DIM ** -0.5) for hd in hs]
    k = [_l2n(qkv[:, GDN_WIDTH + hd * GDN_HEAD_DIM:GDN_WIDTH + (hd + 1) * GDN_HEAD_DIM]) for hd in hs]
    v = [qkv[:, 2 * GDN_WIDTH + hd * GDN_HEAD_DIM:2 * GDN_WIDTH + (hd + 1) * GDN_HEAD_DIM] for hd in hs]
    beta = [beta_all[:, hd:hd + 1] for hd in hs]
    gc = [gc_all[:, GDN_HEADS + hd:GDN_HEADS + hd + 1] for hd in hs]
    gl0 = [gc[hd][GDN_CHUNK - 1:GDN_CHUNK, :] for hd in hs]
    gl1 = [gc[hd][ROWS - 1:ROWS, :] for hd in hs]
    glast = [jnp.where(top, gl0[hd], gl1[hd]) for hd in hs]
    sd = [(jnp.exp(gl0[hd]), jnp.exp(gl1[hd])) for hd in hs]
    o, s_new = _gdn_block(q, k, v, beta, gc, glast, sd, [s_ref[hd] for hd in hs])
    for hd in hs:
        s_ref[hd] = s_new[hd]
    za = proj(C_ZA, GDN_WIDTH)
    o_a = [_rms(o[hd], onorm_ref[...]) * _silu(za[:, hsl[hd]]) for hd in hs]

    c_ext[CONF_HALO:CONF_HALO + ROWS, :] = proj(C_GV, CONF_WIDTH) * _sigmoid(proj(C_GG, CONF_WIDTH))
    acc = _causal_conv(c_ext[...], dww_ref, CONF_CONV, CONF_HALO, acc=dwb_ref[...])
    c_ext[0:CONF_HALO, :] = c_ext[ROWS:ROWS + CONF_HALO, :]
    mu = jnp.mean(acc, axis=-1, keepdims=True)
    cen = acc - mu
    var = jnp.mean(cen * cen, axis=-1, keepdims=True)
    ln = cen * lax.rsqrt(var + NORM_EPS) * lnw_ref[...] + lnb_ref[...]
    o_b = _silu(ln) * _silu(proj(C_ZB, CONF_WIDTH))

    mixed = jnp.concatenate(o_a + [o_b], axis=1).astype(jnp.bfloat16)
    y = jnp.dot(mixed, wout_ref[...], preferred_element_type=jnp.float32)
    out_ref[0] = h + _rms(y, post_ref[...])


def _odd_kernel(sinks_ref, h_ref, pre_ref, win_ref, wout_ref, post_ref, out_ref, k2_ref, v2_ref):
    j = pl.program_id(1)
    half = ROWS // 2
    prev0, cur0, tail0 = N_META, N_META + ROWS, N_META + 2 * ROWS
    nwin = 3 * half
    nk = N_META + nwin

    h = h_ref[0]
    hn = _rms(h, pre_ref[...]).astype(jnp.bfloat16)

    def proj(c0, width):
        return jnp.dot(hn, win_ref[:, c0:c0 + width], preferred_element_type=jnp.float32)

    kv = proj(SWA_Q_WIDTH, 2 * SWA_KV_WIDTH)
    lane = lax.broadcasted_iota(jnp.int32, (ROWS, LANES), 1)
    lo = lane < SWA_HEAD_DIM
    kk, vv = kv[:, :LANES], kv[:, LANES:]
    kr = pltpu.roll(kk, SWA_HEAD_DIM, axis=1)
    vr = pltpu.roll(vv, SWA_HEAD_DIM, axis=1)
    k2 = (jnp.where(lo, kk, kr), jnp.where(lo, kr, kk))
    v2 = (jnp.where(lo, vv, vr), jnp.where(lo, vr, vv))

    @pl.when(j == 0)
    def _():
        for g in range(SWA_KV_HEADS):
            for buf, val in ((k2_ref, k2[g]), (v2_ref, v2[g])):
                meta = val[PAD_ROWS:, :].astype(jnp.bfloat16)
                buf[g, 0:N_META, :] = meta
                buf[g, tail0:tail0 + N_META, :] = meta
                buf[g, prev0:cur0, :] = jnp.zeros((ROWS, LANES), jnp.bfloat16)

    for g in range(SWA_KV_HEADS):
        k2_ref[g, cur0:tail0, :] = k2[g].astype(jnp.bfloat16)
        v2_ref[g, cur0:tail0, :] = v2[g].astype(jnp.bfloat16)

    q = proj(0, SWA_Q_WIDTH) * (SWA_HEAD_DIM ** -0.5)
    z = proj(SWA_Q_WIDTH + 2 * SWA_KV_WIDTH, SWA_Q_WIDTH)

    quad = 4
    lo_h = lax.broadcasted_iota(jnp.int32, (half, LANES), 1) < SWA_HEAD_DIM
    o_rows = []
    for u in range(ROWS // half):
        base = 0 if u == 0 else prev0 + half
        r = (lax.broadcasted_iota(jnp.int32, (quad * half, nk), 0) & (half - 1)) + u * half
        c = lax.broadcasted_iota(jnp.int32, (quad * half, nk), 1)
        if u == 0:
            is_meta, m_idx, kp = c < N_META, c, c - N_META - ROWS
        else:
            is_meta, m_idx, kp = c >= nwin, c - nwin, c - half
        valid = ((is_meta & ((j >= 1) | (m_idx <= r - PAD_ROWS)))
                 | (~is_meta & (kp <= r) & (kp > r - SWA_WINDOW)
                    & (((kp < 0) & (j >= 2)) | ((kp >= 0) & (j >= 1)))))
        grp = lax.broadcasted_iota(jnp.int32, (quad * half, 1), 0) // half
        o_pairs = []
        for t in range(SWA_Q_HEADS // quad):
            g = (quad * t) // (SWA_Q_HEADS // SWA_KV_HEADS)
            qa = q[u * half:(u + 1) * half, (2 * t) * LANES:(2 * t + 1) * LANES]
            qb = q[u * half:(u + 1) * half, (2 * t + 1) * LANES:(2 * t + 2) * LANES]
            lhs = jnp.concatenate([jnp.where(lo_h, qa, 0.0), jnp.where(lo_h, 0.0, qa),
                                   jnp.where(lo_h, qb, 0.0), jnp.where(lo_h, 0.0, qb)], axis=0)
            s = lax.dot_general(lhs.astype(jnp.bfloat16), k2_ref[g, base:base + nk, :], _NT,
                                preferred_element_type=jnp.float32)
            s = jnp.where(valid, s, -jnp.inf)
            sink = jnp.where(grp == 0, sinks_ref[quad * t],
                             jnp.where(grp == 1, sinks_ref[quad * t + 1],
                                       jnp.where(grp == 2, sinks_ref[quad * t + 2],
                                                 sinks_ref[quad * t + 3])))
            m = jnp.maximum(jnp.max(s, axis=-1, keepdims=True), sink)
            e = jnp.exp(s - m)
            den = jnp.sum(e, axis=-1, keepdims=True) + jnp.exp(sink - m)
            o4 = jnp.dot((e / den).astype(jnp.bfloat16), v2_ref[g, base:base + nk, :],
                         preferred_element_type=jnp.float32)
            o_pairs.append(jnp.where(lo_h, o4[0:half], o4[half:2 * half]))
            o_pairs.append(jnp.where(lo_h, o4[2 * half:3 * half], o4[3 * half:]))
        o_rows.append(jnp.concatenate(o_pairs, axis=1))

    for g in range(SWA_KV_HEADS):
        k2_ref[g, prev0:cur0, :] = k2_ref[g, cur0:tail0, :]
        v2_ref[g, prev0:cur0, :] = v2_ref[g, cur0:tail0, :]

    gated = (jnp.concatenate(o_rows, axis=0) * _silu(z)).astype(jnp.bfloat16)
    y = jnp.dot(gated, wout_ref[...], preferred_element_type=jnp.float32)
    out_ref[0] = h + _rms(y, post_ref[...])


def _const_spec(shape):
    return pl.BlockSpec(shape, lambda b, j: (0,) * len(shape))


def _lane_pad(v, offset):
    return jnp.zeros((1, LANES), jnp.float32).at[0, offset:offset + v.shape[0]].set(v)


def kernel(x, meta_tokens, even_pre_norm, even_w_in, even_qkv_conv, even_a_log, even_dt_bias,
           even_out_norm, even_dw_conv, even_dw_bias, even_ln_w, even_ln_b, even_w_out,
           even_post_norm, odd_pre_norm, odd_w_in, odd_sinks, odd_w_out, odd_post_norm):
    bsz, seq, d = x.shape
    assert d == D_MODEL and seq % ROWS == 0
    nblk = seq // ROWS + 1
    f32 = jnp.float32

    w0 = even_w_in[0]
    s_qkv, s_za, s_b, s_a, s_gv, s_gg, s_zb = jnp.split(
        w0, [1536, 2048, 2052, 2056, 2568, 3080], axis=1)
    gate_cols = jnp.concatenate(
        [s_b, s_a, jnp.zeros((d, LANES - 2 * GDN_HEADS), f32)], axis=1)
    w_in0 = jnp.concatenate([s_qkv, s_za, s_gv, s_gg, s_zb, gate_cols], axis=1).astype(jnp.bfloat16)
    w_out0 = even_w_out[0].astype(jnp.bfloat16)
    w_in1 = odd_w_in[0].astype(jnp.bfloat16)
    w_out1 = odd_w_out[0].astype(jnp.bfloat16)
    meta_blk = jnp.concatenate([jnp.zeros((PAD_ROWS, d), f32), meta_tokens.astype(f32)], axis=0)
    alog = _lane_pad(even_a_log[0], GDN_HEADS)
    dtb = _lane_pad(even_dt_bias[0], GDN_HEADS)
    row = lambda v: v.reshape(1, -1).astype(f32)

    params = pltpu.CompilerParams(dimension_semantics=("parallel", "arbitrary"),
                                  vmem_limit_bytes=VMEM_LIMIT_BYTES)

    h1 = pl.pallas_call(
        _even_kernel,
        grid=(bsz, nblk),
        in_specs=[
            pl.BlockSpec((1, ROWS, d), lambda b, j: (b, jnp.maximum(j - 1, 0), 0)),
            _const_spec((ROWS, d)),
            _const_spec((1, d)),
            _const_spec((d, EVEN_COLS)),
            _const_spec((GDN_CONV, 3 * GDN_WIDTH)),
            _const_spec((1, LANES)),
            _const_spec((1, LANES)),
            _const_spec((1, GDN_HEAD_DIM)),
            _const_spec((CONF_CONV, CONF_WIDTH)),
            _const_spec((1, CONF_WIDTH)),
            _const_spec((1, CONF_WIDTH)),
            _const_spec((1, CONF_WIDTH)),
            _const_spec((d, d)),
            _const_spec((1, d)),
        ],
        out_specs=pl.BlockSpec((1, ROWS, d), lambda b, j: (b, j, 0)),
        out_shape=jax.ShapeDtypeStruct((bsz, nblk * ROWS, d), f32),
        scratch_shapes=[
            pltpu.VMEM((QKV_HALO + ROWS, 3 * GDN_WIDTH), f32),
            pltpu.VMEM((CONF_HALO + ROWS, CONF_WIDTH), f32),
            pltpu.VMEM((GDN_HEADS, GDN_HEAD_DIM, GDN_HEAD_DIM), f32),
        ],
        compiler_params=params,
        name="even_layer",
    )(x, meta_blk, row(even_pre_norm[0]), w_in0, even_qkv_conv[0].astype(f32), alog, dtb,
      row(even_out_norm[0]), even_dw_conv[0].astype(f32), row(even_dw_bias[0]),
      row(even_ln_w[0]), row(even_ln_b[0]), w_out0, row(even_post_norm[0]))

    nk = 2 * N_META + 2 * ROWS
    out = pl.pallas_call(
        _odd_kernel,
        grid_spec=pltpu.PrefetchScalarGridSpec(
            num_scalar_prefetch=1,
            grid=(bsz, nblk),
            in_specs=[
                pl.BlockSpec((1, ROWS, d), lambda b, j, s: (b, j, 0)),
                pl.BlockSpec((1, d), lambda b, j, s: (0, 0)),
                pl.BlockSpec((d, 2 * SWA_Q_WIDTH + 2 * SWA_KV_WIDTH), lambda b, j, s: (0, 0)),
                pl.BlockSpec((SWA_Q_WIDTH, d), lambda b, j, s: (0, 0)),
                pl.BlockSpec((1, d), lambda b, j, s: (0, 0)),
            ],
            out_specs=pl.BlockSpec((1, ROWS, d), lambda b, j, s: (b, jnp.maximum(j - 1, 0), 0)),
            scratch_shapes=[
                pltpu.VMEM((SWA_KV_HEADS, nk, LANES), jnp.bfloat16),
                pltpu.VMEM((SWA_KV_HEADS, nk, LANES), jnp.bfloat16),
            ],
        ),
        out_shape=jax.ShapeDtypeStruct((bsz, seq, d), f32),
        compiler_params=params,
        name="odd_layer",
    )(odd_sinks[0].astype(f32), h1, row(odd_pre_norm[0]), w_in1, w_out1, row(odd_post_norm[0]))
    return out
```

```python
import functools
import types

import jax
import jax.numpy as jnp
from jax import lax
from jax.experimental import pallas as pl
from jax.experimental.pallas import tpu as pltpu

D_MODEL = 1024
N_META = 16
NORM_EPS = 1e-6
NEG_LOG2_E = -1.4426950408889634

GDN_HEADS = 4
GDN_HEAD_DIM = 128
GDN_WIDTH = GDN_HEADS * GDN_HEAD_DIM
GDN_CONV = 4
GDN_CHUNK = 64
CONF_WIDTH = D_MODEL - GDN_WIDTH
CONF_CONV = 31

SWA_HEAD_DIM = 64
SWA_Q_HEADS = D_MODEL // SWA_HEAD_DIM
SWA_KV_HEADS = 2
SWA_WINDOW = 128
SWA_Q_WIDTH = SWA_Q_HEADS * SWA_HEAD_DIM
SWA_KV_WIDTH = SWA_KV_HEADS * SWA_HEAD_DIM
ODD_COLS = 2 * SWA_Q_WIDTH + 2 * SWA_KV_WIDTH

LANES = 128
SUBLANES = 8
ROWS = 128
HALF = ROWS // 2
SEQS = 2
PAD_ROWS = ROWS - N_META

C_QKV = 0
C_ZA = C_QKV + 3 * GDN_WIDTH
C_GV = C_ZA + GDN_WIDTH
C_GG = C_GV + CONF_WIDTH
C_ZB = C_GG + CONF_WIDTH
C_BA = C_ZB + CONF_WIDTH
EVEN_COLS = C_BA + LANES

QKV_HALO = 8
CONF_HALO = 32

KV_PREV, KV_CUR, KV_TAIL = N_META, N_META + ROWS, N_META + 2 * ROWS
KV_ROWS = KV_TAIL + N_META
N_WIN = 3 * HALF
N_KEYS = N_META + N_WIN

VMEM_LIMIT_BYTES = 56 * 1024 * 1024

_NT = (((1,), (1,)), ((), ()))


def _dot(a, b):
    return jnp.dot(a.astype(jnp.bfloat16), b.astype(jnp.bfloat16), preferred_element_type=jnp.float32)


def _dot_nt(a, b):
    return lax.dot_general(a.astype(jnp.bfloat16), b.astype(jnp.bfloat16), _NT,
                           preferred_element_type=jnp.float32)


def _rms(x, w):
    return x * lax.rsqrt(jnp.mean(x * x, axis=-1, keepdims=True) + NORM_EPS) * w


def _sigmoid(x):
    return 1.0 / (1.0 + jnp.exp2(x * NEG_LOG2_E))


def _silu(x):
    return x * _sigmoid(x)


def _softplus(x):
    return jnp.maximum(x, 0.0) + jnp.log(1.0 + jnp.exp(-jnp.abs(x)))


def _l2n(x):
    return x * lax.rsqrt(jnp.sum(x * x, axis=-1, keepdims=True) + NORM_EPS)


def _interleave(*stages):
    live = list(stages)
    while live:
        for g in list(live):
            try:
                next(g)
            except StopIteration:
                live.remove(g)


class _RowProj:
    def __init__(self, blocks, w_ref):
        self.lhs = blocks[0] if len(blocks) == 1 else jnp.concatenate(blocks, axis=0)
        self.w_ref = w_ref
        self.cache = {}

    def __call__(self, s, c0, width):
        if (c0, width) not in self.cache:
            self.cache[(c0, width)] = jnp.dot(self.lhs, self.w_ref[:, c0:c0 + width],
                                              preferred_element_type=jnp.float32)
        return self.cache[(c0, width)][s * ROWS:(s + 1) * ROWS]


def _gdn_stages(e, s_ref, res):
    n = ROWS
    q, k, v, beta, gc, glast, sd = e.q, e.k, e.v, e.beta, e.gc, e.glast, e.sd
    heads = range(len(q))
    row = lax.broadcasted_iota(jnp.int32, (n, n), 0)
    col = lax.broadcasted_iota(jnp.int32, (n, n), 1)
    same = (row >= GDN_CHUNK) == (col >= GDN_CHUNK)
    incl = same & (col <= row)
    strict = same & (col < row)
    eye = row == col
    ident = jnp.where(eye, 1.0, 0.0)

    kb = [k[h] * beta[h] for h in heads]
    kq = [_dot_nt(jnp.concatenate([kb[h], q[h]], axis=0), k[h]) for h in heads]
    yield

    gc_b = [jnp.broadcast_to(gc[h], (n, n)) for h in heads]
    gc_row = [jnp.sum(jnp.where(eye, gc_b[h], 0.0), axis=0, keepdims=True) for h in heads]
    decay = [jnp.exp(jnp.where(incl, gc_b[h] - gc_row[h], 0.0)) for h in heads]
    a = [kq[h][:n] * jnp.where(strict, decay[h], 0.0) for h in heads]
    qk = [kq[h][n:] * jnp.where(incl, decay[h], 0.0) for h in heads]

    p = [ident - a[h] for h in heads]
    bm = [_dot(a[h], a[h]) for h in heads]
    yield
    levels = GDN_CHUNK.bit_length() - 1
    for lvl in range(1, levels):
        if lvl < levels - 1:
            r = [_dot(bm[h], jnp.concatenate([bm[h], p[h]], axis=1)) for h in heads]
            bm = [r[h][:, :n] for h in heads]
            p = [p[h] + r[h][:, n:] for h in heads]
        else:
            p = [p[h] + _dot(bm[h], p[h]) for h in heads]
        yield

    egc = [jnp.exp(gc[h]) for h in heads]
    uw = [_dot(p[h], jnp.concatenate([v[h] * beta[h], kb[h] * egc[h]], axis=1)) for h in heads]
    qd = [q[h] * egc[h] for h in heads]
    kd_t = [jnp.transpose(k[h] * jnp.exp(glast[h] - gc[h])) for h in heads]
    yield

    outs = [[] for _ in heads]
    state = [s_ref[h] for h in heads]
    zeros = jnp.zeros((GDN_CHUNK, GDN_HEAD_DIM), jnp.float32)
    for c in range(n // GDN_CHUNK):
        sl = slice(c * GDN_CHUNK, (c + 1) * GDN_CHUNK)
        r1 = [_dot(jnp.concatenate([uw[h][sl, GDN_HEAD_DIM:], qd[h][sl]], axis=0), state[h])
              for h in heads]
        yield
        v_new = [uw[h][sl, :GDN_HEAD_DIM] - r1[h][:GDN_CHUNK] for h in heads]
        v_full = [jnp.concatenate([v_new[h], zeros] if c == 0 else [zeros, v_new[h]], axis=0)
                  for h in heads]
        r2 = [_dot(jnp.concatenate([qk[h][sl], kd_t[h]], axis=0), v_full[h]) for h in heads]
        for h in heads:
            outs[h].append(r1[h][GDN_CHUNK:] + r2[h][:GDN_CHUNK])
            state[h] = state[h] * sd[h][c] + r2[h][GDN_CHUNK:]
        yield
    for h in heads:
        s_ref[h] = state[h]
    res["o"] = [jnp.concatenate(o, axis=0) for o in outs]


def _causal_conv_stages(ext_ref, w_ref, taps, halo, lanes, res, bias_ref=None):
    ext = ext_ref[:, lanes]
    n = ext.shape[0]
    first = halo - (taps - 1)
    acc = None if bias_ref is None else bias_ref[:, lanes]
    groups = 0
    for phase in range(SUBLANES):
        starts = [first + t for t in range(taps) if (first + t) % SUBLANES == phase]
        if not starts:
            continue
        shifted = ext if phase == 0 else pltpu.roll(ext, n - phase, axis=0)
        for st in starts:
            t = st - first
            term = w_ref[t:t + 1, lanes] * shifted[st - phase:st - phase + ROWS, :]
            acc = term if acc is None else acc + term
        groups += 1
        if groups % 2 == 0:
            yield
    res["conv"] = acc


def _run(gen):
    for _ in gen:
        pass


def _even_front(proj, pad_rows, p, qkv_ext):
    e = types.SimpleNamespace()
    qkv_ext[QKV_HALO:QKV_HALO + ROWS, :] = proj(C_QKV, 3 * GDN_WIDTH)
    res = {}
    qkv = []
    for c in range(3 * GDN_HEADS):
        lanes = slice(c * LANES, (c + 1) * LANES)
        _run(_causal_conv_stages(qkv_ext, p.qkvw, GDN_CONV, QKV_HALO, lanes, res))
        qkv.append(_silu(res["conv"]))
    qkv_ext[0:QKV_HALO, :] = qkv_ext[ROWS:ROWS + QKV_HALO, :]

    gate = proj(C_BA, LANES)
    rowi = lax.broadcasted_iota(jnp.int32, (ROWS, LANES), 0)
    beta_all = _sigmoid(gate)
    g_all = -jnp.exp(p.alog[...]) * _softplus(gate + p.dtb[...])
    if pad_rows:
        g_all = jnp.where(rowi >= pad_rows, g_all, 0.0)
    r2 = lax.broadcasted_iota(jnp.int32, (ROWS, ROWS), 0)
    c2 = lax.broadcasted_iota(jnp.int32, (ROWS, ROWS), 1)
    tri = jnp.where(((r2 >= GDN_CHUNK) == (c2 >= GDN_CHUNK)) & (c2 <= r2), 1.0, 0.0)
    gc_all = jnp.dot(tri, g_all, preferred_element_type=jnp.float32,
                     precision=lax.Precision.HIGHEST)

    top = rowi[:, 0:1] < GDN_CHUNK
    hs = range(GDN_HEADS)
    e.hsl = [slice(hd * GDN_HEAD_DIM, (hd + 1) * GDN_HEAD_DIM) for hd in hs]
    e.q = [_l2n(qkv[hd]) * (GDN_HEAD_DIM ** -0.5) for hd in hs]
    e.k = [_l2n(qkv[GDN_HEADS + hd]) for hd in hs]
    e.v = [qkv[2 * GDN_HEADS + hd] for hd in hs]
    e.beta = [beta_all[:, hd:hd + 1] for hd in hs]
    e.gc = [gc_all[:, GDN_HEADS + hd:GDN_HEADS + hd + 1] for hd in hs]
    gl0 = [e.gc[hd][GDN_CHUNK - 1:GDN_CHUNK, :] for hd in hs]
    gl1 = [e.gc[hd][ROWS - 1:ROWS, :] for hd in hs]
    e.glast = [jnp.where(top, gl0[hd], gl1[hd]) for hd in hs]
    e.sd = [(jnp.exp(gl0[hd]), jnp.exp(gl1[hd])) for hd in hs]
    return e


def _conformer_stages(proj, p, c_ext, za_ref, put_ob):
    c_ext[CONF_HALO:CONF_HALO + ROWS, :] = proj(C_GV, CONF_WIDTH) * _sigmoid(proj(C_GG, CONF_WIDTH))
    yield
    res = {}
    zb = None
    chunks = []
    for c in range(CONF_WIDTH // LANES):
        lanes = slice(c * LANES, (c + 1) * LANES)
        yield from _causal_conv_stages(c_ext, p.dww, CONF_CONV, CONF_HALO, lanes, res, p.dwb)
        chunks.append(res["conv"])
        if c == 0:
            za_ref[...] = _silu(proj(C_ZA, GDN_WIDTH))
        if c == 2:
            zb = _silu(proj(C_ZB, CONF_WIDTH))
        yield
    acc = jnp.concatenate(chunks, axis=1)
    c_ext[0:CONF_HALO, :] = c_ext[ROWS:ROWS + CONF_HALO, :]
    mu = jnp.mean(acc, axis=-1, keepdims=True)
    cen = acc - mu
    var = jnp.mean(cen * cen, axis=-1, keepdims=True)
    ln = cen * lax.rsqrt(var + NORM_EPS) * p.lnw[...] + p.lnb[...]
    put_ob(_silu(ln) * zb)


def _gdn_out(e, p, res, za_ref, put_oa):
    for hd in range(GDN_HEADS):
        put_oa(hd, _rms(res["o"][hd], p.onorm[...]) * za_ref[:, e.hsl[hd]])


def _dup_kv(kv):
    lo = lax.broadcasted_iota(jnp.int32, (ROWS, LANES), 1) < SWA_HEAD_DIM
    kk, vv = kv[:, :LANES], kv[:, LANES:]
    kr = pltpu.roll(kk, SWA_HEAD_DIM, axis=1)
    vr = pltpu.roll(vv, SWA_HEAD_DIM, axis=1)
    k2 = (jnp.where(lo, kk, kr), jnp.where(lo, kr, kk))
    v2 = (jnp.where(lo, vv, vr), jnp.where(lo, vr, vv))
    return k2, v2


def _attention_stages(q_ref, has_prev, sinks_ref, k2_ref, v2_ref, z_ref, put):
    quad = 4
    lo_h = lax.broadcasted_iota(jnp.int32, (HALF, LANES), 1) < SWA_HEAD_DIM
    for u in range(ROWS // HALF):
        base = 0 if u == 0 else KV_PREV + HALF
        rows = slice(u * HALF, (u + 1) * HALF)
        r = (lax.broadcasted_iota(jnp.int32, (quad * HALF, N_KEYS), 0) & (HALF - 1)) + u * HALF
        c = lax.broadcasted_iota(jnp.int32, (quad * HALF, N_KEYS), 1)
        if u == 0:
            is_meta, kp = c < N_META, c - N_META - ROWS
        else:
            is_meta, kp = c >= N_WIN, c - HALF
        valid = is_meta | ((kp <= r) & (kp > r - SWA_WINDOW) & ((kp >= 0) | has_prev))
        grp = lax.broadcasted_iota(jnp.int32, (quad * HALF, 1), 0) // HALF
        for t in range(SWA_Q_HEADS // quad):
            g = (quad * t) // (SWA_Q_HEADS // SWA_KV_HEADS)
            la = slice((2 * t) * LANES, (2 * t + 1) * LANES)
            lb = slice((2 * t + 1) * LANES, (2 * t + 2) * LANES)
            qa, qb = q_ref[rows, la], q_ref[rows, lb]
            lhs = jnp.concatenate([jnp.where(lo_h, qa, 0.0), jnp.where(lo_h, 0.0, qa),
                                   jnp.where(lo_h, qb, 0.0), jnp.where(lo_h, 0.0, qb)], axis=0)
            s = lax.dot_general(lhs.astype(jnp.bfloat16), k2_ref[g, base:base + N_KEYS, :], _NT,
                                preferred_element_type=jnp.float32)
            s = jnp.where(valid, s, -jnp.inf)
            sink = jnp.where(grp == 0, sinks_ref[quad * t],
                             jnp.where(grp == 1, sinks_ref[quad * t + 1],
                                       jnp.where(grp == 2, sinks_ref[quad * t + 2],
                                                 sinks_ref[quad * t + 3])))
            m = jnp.maximum(jnp.max(s, axis=-1, keepdims=True), sink)
            ex = jnp.exp(s - m)
            den = jnp.sum(ex, axis=-1, keepdims=True) + jnp.exp(sink - m)
            o4 = jnp.dot((ex / den).astype(jnp.bfloat16), v2_ref[g, base:base + N_KEYS, :],
                         preferred_element_type=jnp.float32)
            put(u, 2 * t, jnp.where(lo_h, o4[0:HALF], o4[HALF:2 * HALF]) * z_ref[rows, la])
            put(u, 2 * t + 1, jnp.where(lo_h, o4[2 * HALF:3 * HALF], o4[3 * HALF:]) * z_ref[rows, lb])
            yield


def _even_params(pre, win, qkvw, alog, dtb, onorm, dww, dwb, lnw, lnb, wout, post):
    return types.SimpleNamespace(pre=pre, win=win, qkvw=qkvw, alog=alog, dtb=dtb, onorm=onorm,
                                 dww=dww, dwb=dwb, lnw=lnw, lnb=lnb, wout=wout, post=post)


N_EVEN_PARAMS = 12


def _prologue_kernel(meta_ref, *refs):
    p = _even_params(*refs[:N_EVEN_PARAMS])
    (pre1, wkv1, s0_out, qkvt_out, ct_out, kmeta_out, vmeta_out,
     qkv_ext, c_ext, za_s, mixed_s) = refs[N_EVEN_PARAMS:]
    qkv_ext[0:QKV_HALO, :] = jnp.zeros((QKV_HALO, 3 * GDN_WIDTH), jnp.float32)
    c_ext[0:CONF_HALO, :] = jnp.zeros((CONF_HALO, CONF_WIDTH), jnp.float32)
    s0_out[...] = jnp.zeros(s0_out.shape, jnp.float32)

    h = meta_ref[...]
    proj = functools.partial(_RowProj([_rms(h, p.pre[...]).astype(jnp.bfloat16)], p.win), 0)

    def put_ob(val):
        mixed_s[:, GDN_WIDTH:] = val.astype(jnp.bfloat16)

    def put_oa(hd, val):
        mixed_s[:, hd * GDN_HEAD_DIM:(hd + 1) * GDN_HEAD_DIM] = val.astype(jnp.bfloat16)

    e = _even_front(proj, PAD_ROWS, p, qkv_ext)
    res = {}
    _interleave(_gdn_stages(e, s0_out, res), _conformer_stages(proj, p, c_ext, za_s, put_ob))
    _gdn_out(e, p, res, za_s, put_oa)
    y0 = jnp.dot(mixed_s[...], p.wout[...], preferred_element_type=jnp.float32)
    h1 = h + _rms(y0, p.post[...])
    qkvt_out[...] = qkv_ext[0:QKV_HALO, :]
    ct_out[...] = c_ext[0:CONF_HALO, :]

    hn1 = _rms(h1, pre1[...]).astype(jnp.bfloat16)
    k2, v2 = _dup_kv(jnp.dot(hn1, wkv1[...], preferred_element_type=jnp.float32))
    for g in range(SWA_KV_HEADS):
        kmeta_out[g] = k2[g][PAD_ROWS:, :].astype(jnp.bfloat16)
        vmeta_out[g] = v2[g][PAD_ROWS:, :].astype(jnp.bfloat16)


def _main_kernel(sinks_ref, x_ref, *refs):
    p = _even_params(*refs[:N_EVEN_PARAMS])
    (s0_ref, qkvt_ref, ct_ref, pre1, win1, wout1, post1, kmeta_ref, vmeta_ref,
     out_ref, qkv_ext, c_ext, s_ref, h1_s, k2_ref, v2_ref,
     q1_s, z1_s, za_s, mixed_s, gated_s) = refs[N_EVEN_PARAMS:]
    j = pl.program_id(1)
    seqs = range(SEQS)

    @pl.when(j == 0)
    def _():
        h1_s[...] = jnp.zeros(h1_s.shape, jnp.float32)
        for s in seqs:
            qkv_ext[s, 0:QKV_HALO, :] = qkvt_ref[...]
            c_ext[s, 0:CONF_HALO, :] = ct_ref[...]
            s_ref[s] = s0_ref[...]
            for g in range(SWA_KV_HEADS):
                for buf, meta in ((k2_ref, kmeta_ref), (v2_ref, vmeta_ref)):
                    buf[s, g, 0:N_META, :] = meta[g]
                    buf[s, g, KV_TAIL:KV_ROWS, :] = meta[g]
                    buf[s, g, KV_PREV:KV_CUR, :] = jnp.zeros((ROWS, LANES), jnp.bfloat16)

    h1 = [h1_s[s] for s in seqs]
    h = [x_ref[s] for s in seqs]
    proj1 = _RowProj([_rms(h1[s], pre1[...]).astype(jnp.bfloat16) for s in seqs], win1)
    proj0 = _RowProj([_rms(h[s], p.pre[...]).astype(jnp.bfloat16) for s in seqs], p.win)

    def put_ob(s, val):
        mixed_s[s * ROWS:(s + 1) * ROWS, GDN_WIDTH:] = val.astype(jnp.bfloat16)

    def put_oa(s, hd, val):
        mixed_s[s * ROWS:(s + 1) * ROWS, hd * GDN_HEAD_DIM:(hd + 1) * GDN_HEAD_DIM] = val.astype(jnp.bfloat16)

    def put_gated(s, u, pair, val):
        gated_s[s * ROWS + u * HALF:s * ROWS + (u + 1) * HALF,
                pair * LANES:(pair + 1) * LANES] = val.astype(jnp.bfloat16)

    for s in seqs:
        k2, v2 = _dup_kv(proj1(s, SWA_Q_WIDTH, 2 * SWA_KV_WIDTH))
        for g in range(SWA_KV_HEADS):
            k2_ref[s, g, KV_CUR:KV_TAIL, :] = k2[g].astype(jnp.bfloat16)
            v2_ref[s, g, KV_CUR:KV_TAIL, :] = v2[g].astype(jnp.bfloat16)
    for s in seqs:
        q1_s[s] = proj1(s, 0, SWA_Q_WIDTH) * (SWA_HEAD_DIM ** -0.5)
    for s in seqs:
        z1_s[s] = _silu(proj1(s, SWA_Q_WIDTH + 2 * SWA_KV_WIDTH, SWA_Q_WIDTH))

    e = [_even_front(functools.partial(proj0, s), 0, p, qkv_ext.at[s]) for s in seqs]
    res = [{} for _ in seqs]

    def gdn_stages(s):
        yield from _gdn_stages(e[s], s_ref.at[s], res[s])
        _gdn_out(e[s], p, res[s], za_s.at[s], functools.partial(put_oa, s))

    _interleave(*[gdn_stages(s) for s in seqs],
                *[_attention_stages(q1_s.at[s], j >= 2, sinks_ref, k2_ref.at[s], v2_ref.at[s],
                                    z1_s.at[s], functools.partial(put_gated, s)) for s in seqs],
                *[_conformer_stages(functools.partial(proj0, s), p, c_ext.at[s], za_s.at[s],
                                    functools.partial(put_ob, s)) for s in seqs])

    for s in seqs:
        for g in range(SWA_KV_HEADS):
            k2_ref[s, g, KV_PREV:KV_CUR, :] = k2_ref[s, g, KV_CUR:KV_TAIL, :]
            v2_ref[s, g, KV_PREV:KV_CUR, :] = v2_ref[s, g, KV_CUR:KV_TAIL, :]
    y1 = jnp.dot(gated_s[...], wout1[...], preferred_element_type=jnp.float32)
    for s in seqs:
        out_ref[s] = h1[s] + _rms(y1[s * ROWS:(s + 1) * ROWS], post1[...])

    y0 = jnp.dot(mixed_s[...], p.wout[...], preferred_element_type=jnp.float32)
    for s in seqs:
        h1_s[s] = h[s] + _rms(y0[s * ROWS:(s + 1) * ROWS], p.post[...])


def _lane_pad(v, offset):
    return jnp.zeros((1, LANES), jnp.float32).at[0, offset:offset + v.shape[0]].set(v)


def kernel(x, meta_tokens, even_pre_norm, even_w_in, even_qkv_conv, even_a_log, even_dt_bias,
           even_out_norm, even_dw_conv, even_dw_bias, even_ln_w, even_ln_b, even_w_out,
           even_post_norm, odd_pre_norm, odd_w_in, odd_sinks, odd_w_out, odd_post_norm):
    bsz, seq, d = x.shape
    assert d == D_MODEL and seq % ROWS == 0 and bsz % SEQS == 0
    nblk = seq // ROWS
    f32, bf16 = jnp.float32, jnp.bfloat16

    w0 = even_w_in[0]
    s_qkv, s_za, s_b, s_a, s_gv, s_gg, s_zb = jnp.split(
        w0, [1536, 2048, 2052, 2056, 2568, 3080], axis=1)
    gate_cols = jnp.concatenate(
        [s_b, s_a, jnp.zeros((d, LANES - 2 * GDN_HEADS), f32)], axis=1)
    w_in0 = jnp.concatenate([s_qkv, s_za, s_gv, s_gg, s_zb, gate_cols], axis=1).astype(bf16)
    w_out0 = even_w_out[0].astype(bf16)
    w_in1 = odd_w_in[0].astype(bf16)
    w_kv1 = w_in1[:, SWA_Q_WIDTH:SWA_Q_WIDTH + 2 * SWA_KV_WIDTH]
    w_out1 = odd_w_out[0].astype(bf16)
    meta_blk = jnp.concatenate([jnp.zeros((PAD_ROWS, d), f32), meta_tokens.astype(f32)], axis=0)
    row = lambda v: v.reshape(1, -1).astype(f32)
    even_args = (row(even_pre_norm[0]), w_in0, even_qkv_conv[0].astype(f32),
                 _lane_pad(even_a_log[0], GDN_HEADS), _lane_pad(even_dt_bias[0], GDN_HEADS),
                 row(even_out_norm[0]), even_dw_conv[0].astype(f32), row(even_dw_bias[0]),
                 row(even_ln_w[0]), row(even_ln_b[0]), w_out0, row(even_post_norm[0]))
    even_shapes = [a.shape for a in even_args]

    state_shapes = [
        jax.ShapeDtypeStruct((GDN_HEADS, GDN_HEAD_DIM, GDN_HEAD_DIM), f32),
        jax.ShapeDtypeStruct((QKV_HALO, 3 * GDN_WIDTH), f32),
        jax.ShapeDtypeStruct((CONF_HALO, CONF_WIDTH), f32),
        jax.ShapeDtypeStruct((SWA_KV_HEADS, N_META, LANES), bf16),
        jax.ShapeDtypeStruct((SWA_KV_HEADS, N_META, LANES), bf16),
    ]

    def full1(shape):
        return pl.BlockSpec(shape, lambda i: (0,) * len(shape))

    s0, qkv_tail, c_tail, k_meta, v_meta = pl.pallas_call(
        _prologue_kernel,
        grid=(1,),
        in_specs=[full1((ROWS, d))] + [full1(s) for s in even_shapes]
        + [full1((1, d)), full1((d, 2 * SWA_KV_WIDTH))],
        out_specs=[full1(s.shape) for s in state_shapes],
        out_shape=state_shapes,
        scratch_shapes=[pltpu.VMEM((QKV_HALO + ROWS, 3 * GDN_WIDTH), f32),
                        pltpu.VMEM((CONF_HALO + ROWS, CONF_WIDTH), f32),
                        pltpu.VMEM((ROWS, GDN_WIDTH), f32),
                        pltpu.VMEM((ROWS, d), bf16)],
        compiler_params=pltpu.CompilerParams(dimension_semantics=("arbitrary",),
                                             vmem_limit_bytes=VMEM_LIMIT_BYTES),
        name="meta_block",
    )(meta_blk, *even_args, row(odd_pre_norm[0]), w_kv1)

    def full2(shape):
        return pl.BlockSpec(shape, lambda b, j, s: (0,) * len(shape))

    out = pl.pallas_call(
        _main_kernel,
        grid_spec=pltpu.PrefetchScalarGridSpec(
            num_scalar_prefetch=1,
            grid=(bsz // SEQS, nblk + 1),
            in_specs=[pl.BlockSpec((SEQS, ROWS, d), lambda b, j, s: (b, jnp.minimum(j, nblk - 1), 0))]
            + [full2(s) for s in even_shapes]
            + [full2(s.shape) for s in state_shapes[:3]]
            + [full2((1, d)), full2((d, ODD_COLS)), full2((SWA_Q_WIDTH, d)), full2((1, d))]
            + [full2(s.shape) for s in state_shapes[3:]],
            out_specs=pl.BlockSpec((SEQS, ROWS, d), lambda b, j, s: (b, jnp.maximum(j - 1, 0), 0)),
            scratch_shapes=[
                pltpu.VMEM((SEQS, QKV_HALO + ROWS, 3 * GDN_WIDTH), f32),
                pltpu.VMEM((SEQS, CONF_HALO + ROWS, CONF_WIDTH), f32),
                pltpu.VMEM((SEQS, GDN_HEADS, GDN_HEAD_DIM, GDN_HEAD_DIM), f32),
                pltpu.VMEM((SEQS, ROWS, d), f32),
                pltpu.VMEM((SEQS, SWA_KV_HEADS, KV_ROWS, LANES), bf16),
                pltpu.VMEM((SEQS, SWA_KV_HEADS, KV_ROWS, LANES), bf16),
                pltpu.VMEM((SEQS, ROWS, SWA_Q_WIDTH), f32),
                pltpu.VMEM((SEQS, ROWS, SWA_Q_WIDTH), f32),
                pltpu.VMEM((SEQS, ROWS, GDN_WIDTH), f32),
                pltpu.VMEM((SEQS * ROWS, d), bf16),
                pltpu.VMEM((SEQS * ROWS, d), bf16),
            ],
        ),
        out_shape=jax.ShapeDtypeStruct((bsz, seq, d), f32),
        compiler_params=pltpu.CompilerParams(dimension_semantics=("parallel", "arbitrary"),
                                             vmem_limit_bytes=VMEM_LIMIT_BYTES),
        name="trunk",
    )(odd_sinks[0].astype(f32), x, *even_args, s0, qkv_tail, c_tail,
      row(odd_pre_norm[0]), w_in1, w_out1, row(odd_post_norm[0]), k_meta, v_meta)
    return out
```

```python
import functools
import types

import jax
import jax.numpy as jnp
from jax import lax
from jax.experimental import pallas as pl
from jax.experimental.pallas import tpu as pltpu

D_MODEL = 1024
N_META = 16
NORM_EPS = 1e-6
LOG2_E = 1.4426950408889634
NEG_LOG2_E = -LOG2_E

GDN_HEADS = 4
GDN_HEAD_DIM = 128
GDN_WIDTH = GDN_HEADS * GDN_HEAD_DIM
GDN_CONV = 4
GDN_CHUNK = 64
CONF_WIDTH = D_MODEL - GDN_WIDTH
CONF_CONV = 31

SWA_HEAD_DIM = 64
SWA_Q_HEADS = D_MODEL // SWA_HEAD_DIM
SWA_KV_HEADS = 2
SWA_WINDOW = 128
SWA_Q_WIDTH = SWA_Q_HEADS * SWA_HEAD_DIM
SWA_KV_WIDTH = SWA_KV_HEADS * SWA_HEAD_DIM
ODD_COLS = 2 * SWA_Q_WIDTH + 2 * SWA_KV_WIDTH

LANES = 128
SUBLANES = 8
ROWS = 128
HALF = ROWS // 2
SEQS = 2
PAD_ROWS = ROWS - N_META

C_QKV = 0
C_ZA = C_QKV + 3 * GDN_WIDTH
C_GV = C_ZA + GDN_WIDTH
C_GG = C_GV + CONF_WIDTH
C_ZB = C_GG + CONF_WIDTH
C_BA = C_ZB + CONF_WIDTH
EVEN_COLS = C_BA + LANES

QKV_HALO = 8
CONF_HALO = 32

KV_PREV, KV_CUR, KV_TAIL = N_META, N_META + ROWS, N_META + 2 * ROWS
KV_ROWS = KV_TAIL + N_META
N_WIN = 3 * HALF
N_KEYS = N_META + N_WIN

VMEM_LIMIT_BYTES = 56 * 1024 * 1024

_NT = (((1,), (1,)), ((), ()))


def _dot(a, b):
    return jnp.dot(a.astype(jnp.bfloat16), b.astype(jnp.bfloat16), preferred_element_type=jnp.float32)


def _dot_nt(a, b):
    return lax.dot_general(a.astype(jnp.bfloat16), b.astype(jnp.bfloat16), _NT,
                           preferred_element_type=jnp.float32)


def _rms(x, w):
    return x * lax.rsqrt(jnp.mean(x * x, axis=-1, keepdims=True) + NORM_EPS) * w


def _sigmoid(x):
    return 1.0 / (1.0 + jnp.exp2(x * NEG_LOG2_E))


def _silu(x):
    return x * _sigmoid(x)


def _softplus(x):
    return jnp.maximum(x, 0.0) + jnp.log(1.0 + jnp.exp(-jnp.abs(x)))


def _l2n(x):
    return x * lax.rsqrt(jnp.sum(x * x, axis=-1, keepdims=True) + NORM_EPS)


def _interleave(*stages):
    live = list(stages)
    while live:
        for g in list(live):
            try:
                next(g)
            except StopIteration:
                live.remove(g)


class _RowProj:
    def __init__(self, blocks, w_ref):
        self.lhs = blocks[0] if len(blocks) == 1 else jnp.concatenate(blocks, axis=0)
        self.w_ref = w_ref
        self.cache = {}

    def __call__(self, s, c0, width):
        if (c0, width) not in self.cache:
            self.cache[(c0, width)] = jnp.dot(self.lhs, self.w_ref[:, c0:c0 + width],
                                              preferred_element_type=jnp.float32)
        return self.cache[(c0, width)][s * ROWS:(s + 1) * ROWS]


def _gdn_stages(e, s_ref, res):
    n = ROWS
    q, k, v, beta, gc, glast, sd = e.q, e.k, e.v, e.beta, e.gc, e.glast, e.sd
    heads = range(len(q))
    row = lax.broadcasted_iota(jnp.int32, (n, n), 0)
    col = lax.broadcasted_iota(jnp.int32, (n, n), 1)
    same = (row >= GDN_CHUNK) == (col >= GDN_CHUNK)
    incl = same & (col <= row)
    strict = same & (col < row)
    eye = row == col
    ident = jnp.where(eye, 1.0, 0.0)

    kb = [k[h] * beta[h] for h in heads]
    kq = [_dot_nt(jnp.concatenate([kb[h], q[h]], axis=0), k[h]) for h in heads]
    yield

    gc_b = [jnp.broadcast_to(gc[h], (n, n)) for h in heads]
    gc_row = [jnp.sum(jnp.where(eye, gc_b[h], 0.0), axis=0, keepdims=True) for h in heads]
    decay = [jnp.exp(jnp.where(incl, gc_b[h] - gc_row[h], 0.0)) for h in heads]
    a = [kq[h][:n] * jnp.where(strict, decay[h], 0.0) for h in heads]
    qk = [kq[h][n:] * jnp.where(incl, decay[h], 0.0) for h in heads]

    p = [ident - a[h] for h in heads]
    bm = [_dot(a[h], a[h]) for h in heads]
    yield
    levels = GDN_CHUNK.bit_length() - 1
    for lvl in range(1, levels):
        if lvl < levels - 1:
            r = [_dot(bm[h], jnp.concatenate([bm[h], p[h]], axis=1)) for h in heads]
            bm = [r[h][:, :n] for h in heads]
            p = [p[h] + r[h][:, n:] for h in heads]
        else:
            p = [p[h] + _dot(bm[h], p[h]) for h in heads]
        yield

    egc = [jnp.exp(gc[h]) for h in heads]
    uw = [_dot(p[h], jnp.concatenate([v[h] * beta[h], kb[h] * egc[h]], axis=1)) for h in heads]
    qd = [q[h] * egc[h] for h in heads]
    kd_t = [jnp.transpose(k[h] * jnp.exp(glast[h] - gc[h])) for h in heads]
    yield

    outs = [[] for _ in heads]
    state = [s_ref[h] for h in heads]
    zeros = jnp.zeros((GDN_CHUNK, GDN_HEAD_DIM), jnp.float32)
    for c in range(n // GDN_CHUNK):
        sl = slice(c * GDN_CHUNK, (c + 1) * GDN_CHUNK)
        r1 = [_dot(jnp.concatenate([uw[h][sl, GDN_HEAD_DIM:], qd[h][sl]], axis=0), state[h])
              for h in heads]
        yield
        v_new = [uw[h][sl, :GDN_HEAD_DIM] - r1[h][:GDN_CHUNK] for h in heads]
        v_full = [jnp.concatenate([v_new[h], zeros] if c == 0 else [zeros, v_new[h]], axis=0)
                  for h in heads]
        r2 = [_dot(jnp.concatenate([qk[h][sl], kd_t[h]], axis=0), v_full[h]) for h in heads]
        for h in heads:
            outs[h].append(r1[h][GDN_CHUNK:] + r2[h][:GDN_CHUNK])
            state[h] = state[h] * sd[h][c] + r2[h][GDN_CHUNK:]
        yield
    for h in heads:
        s_ref[h] = state[h]
    res["o"] = [jnp.concatenate(o, axis=0) for o in outs]


def _shift_rows_up(x, k):
    n = x.shape[0] // SUBLANES
    keep = lax.broadcasted_iota(jnp.int32, (SUBLANES, x.shape[1]), 0) < SUBLANES - k
    rot = [pltpu.roll(x[i * SUBLANES:(i + 1) * SUBLANES], SUBLANES - k, axis=0) for i in range(n)]
    return jnp.concatenate([jnp.where(keep, rot[i], rot[(i + 1) % n]) for i in range(n)], axis=0)


def _causal_conv_stages(ext_ref, w_ref, taps, halo, lanes, res, bias_ref=None):
    ext = ext_ref[:, lanes]
    n = ext.shape[0]
    first = halo - (taps - 1)
    acc = None if bias_ref is None else bias_ref[:, lanes]
    groups = 0
    for phase in range(SUBLANES):
        starts = [first + t for t in range(taps) if (first + t) % SUBLANES == phase]
        if not starts:
            continue
        shifted = ext if phase == 0 else _shift_rows_up(ext, phase)
        for st in starts:
            t = st - first
            term = w_ref[t:t + 1, lanes] * shifted[st - phase:st - phase + ROWS, :]
            acc = term if acc is None else acc + term
        groups += 1
        if groups % 2 == 0:
            yield
    res["conv"] = acc


def _run(gen):
    for _ in gen:
        pass


def _even_front(proj, pad_rows, p, qkv_ext):
    e = types.SimpleNamespace()
    qkv_ext[QKV_HALO:QKV_HALO + ROWS, :] = proj(C_QKV, 3 * GDN_WIDTH)
    res = {}
    qkv = [None] * (3 * GDN_HEADS)
    for c in [part * GDN_HEADS + hd for hd in range(GDN_HEADS) for part in range(3)]:
        lanes = slice(c * LANES, (c + 1) * LANES)
        _run(_causal_conv_stages(qkv_ext, p.qkvw, GDN_CONV, QKV_HALO, lanes, res))
        qkv[c] = _silu(res["conv"])
    qkv_ext[0:QKV_HALO, :] = qkv_ext[ROWS:ROWS + QKV_HALO, :]

    gate = proj(C_BA, LANES)
    rowi = lax.broadcasted_iota(jnp.int32, (ROWS, LANES), 0)
    beta_all = _sigmoid(gate)
    g_all = -jnp.exp(p.alog[...]) * _softplus(gate + p.dtb[...])
    if pad_rows:
        g_all = jnp.where(rowi >= pad_rows, g_all, 0.0)
    r2 = lax.broadcasted_iota(jnp.int32, (ROWS, ROWS), 0)
    c2 = lax.broadcasted_iota(jnp.int32, (ROWS, ROWS), 1)
    tri = jnp.where(((r2 >= GDN_CHUNK) == (c2 >= GDN_CHUNK)) & (c2 <= r2), 1.0, 0.0)
    gc_all = jnp.dot(tri, g_all, preferred_element_type=jnp.float32,
                     precision=lax.Precision.HIGHEST)

    top = rowi[:, 0:1] < GDN_CHUNK
    hs = range(GDN_HEADS)
    e.hsl = [slice(hd * GDN_HEAD_DIM, (hd + 1) * GDN_HEAD_DIM) for hd in hs]
    e.q = [_l2n(qkv[hd]) * (GDN_HEAD_DIM ** -0.5) for hd in hs]
    e.k = [_l2n(qkv[GDN_HEADS + hd]) for hd in hs]
    e.v = [qkv[2 * GDN_HEADS + hd] for hd in hs]
    e.beta = [beta_all[:, hd:hd + 1] for hd in hs]
    e.gc = [gc_all[:, GDN_HEADS + hd:GDN_HEADS + hd + 1] for hd in hs]
    gl0 = [e.gc[hd][GDN_CHUNK - 1:GDN_CHUNK, :] for hd in hs]
    gl1 = [e.gc[hd][ROWS - 1:ROWS, :] for hd in hs]
    e.glast = [jnp.where(top, gl0[hd], gl1[hd]) for hd in hs]
    e.sd = [(jnp.exp(gl0[hd]), jnp.exp(gl1[hd])) for hd in hs]
    return e


def _conformer_stages(proj, p, c_ext, za_ref, put_ob):
    c_ext[CONF_HALO:CONF_HALO + ROWS, :] = proj(C_GV, CONF_WIDTH) * _sigmoid(proj(C_GG, CONF_WIDTH))
    yield
    res = {}
    zb = None
    chunks = []
    for c in range(CONF_WIDTH // LANES):
        lanes = slice(c * LANES, (c + 1) * LANES)
        yield from _causal_conv_stages(c_ext, p.dww, CONF_CONV, CONF_HALO, lanes, res, p.dwb)
        chunks.append(res["conv"])
        if c == 0:
            za_ref[...] = _silu(proj(C_ZA, GDN_WIDTH))
        if c == 2:
            zb = _silu(proj(C_ZB, CONF_WIDTH))
        yield
    acc = jnp.concatenate(chunks, axis=1)
    c_ext[0:CONF_HALO, :] = c_ext[ROWS:ROWS + CONF_HALO, :]
    mu = jnp.mean(acc, axis=-1, keepdims=True)
    cen = acc - mu
    var = jnp.mean(cen * cen, axis=-1, keepdims=True)
    ln = cen * lax.rsqrt(var + NORM_EPS) * p.lnw[...] + p.lnb[...]
    put_ob(_silu(ln) * zb)


def _gdn_out(e, p, res, za_ref, put_oa):
    for hd in range(GDN_HEADS):
        put_oa(hd, _rms(res["o"][hd], p.onorm[...]) * za_ref[:, e.hsl[hd]])


def _dup_kv(kv):
    lo = lax.broadcasted_iota(jnp.int32, (ROWS, LANES), 1) < SWA_HEAD_DIM
    kk, vv = kv[:, :LANES], kv[:, LANES:]
    kr = pltpu.roll(kk, SWA_HEAD_DIM, axis=1)
    vr = pltpu.roll(vv, SWA_HEAD_DIM, axis=1)
    k2 = (jnp.where(lo, kk, kr), jnp.where(lo, kr, kk))
    v2 = (jnp.where(lo, vv, vr), jnp.where(lo, vr, vv))
    return k2, v2


def _attention_stages(q_ref, has_prev, sinks_ref, k2_ref, v2_ref, z_ref, put):
    quad = 4
    lo_h = lax.broadcasted_iota(jnp.int32, (HALF, LANES), 1) < SWA_HEAD_DIM
    groups = []
    for u in range(ROWS // HALF):
        base = 0 if u == 0 else KV_PREV + HALF
        r = (lax.broadcasted_iota(jnp.int32, (quad * HALF, N_KEYS), 0) & (HALF - 1)) + u * HALF
        c = lax.broadcasted_iota(jnp.int32, (quad * HALF, N_KEYS), 1)
        if u == 0:
            is_meta, kp = c < N_META, c - N_META - ROWS
        else:
            is_meta, kp = c >= N_WIN, c - HALF
        valid = is_meta | ((kp <= r) & (kp > r - SWA_WINDOW) & ((kp >= 0) | has_prev))
        groups.append((slice(u * HALF, (u + 1) * HALF), base, valid))
    grp = lax.broadcasted_iota(jnp.int32, (quad * HALF, 1), 0) // HALF
    for t in range(SWA_Q_HEADS // quad):
        g = (quad * t) // (SWA_Q_HEADS // SWA_KV_HEADS)
        la = slice((2 * t) * LANES, (2 * t + 1) * LANES)
        lb = slice((2 * t + 1) * LANES, (2 * t + 2) * LANES)
        sink = LOG2_E * jnp.where(grp == 0, sinks_ref[quad * t],
                                  jnp.where(grp == 1, sinks_ref[quad * t + 1],
                                            jnp.where(grp == 2, sinks_ref[quad * t + 2],
                                                      sinks_ref[quad * t + 3])))
        out_a, out_b = [], []
        for rows, base, valid in groups:
            qa, qb = q_ref[rows, la], q_ref[rows, lb]
            lhs = jnp.concatenate([jnp.where(lo_h, qa, 0.0), jnp.where(lo_h, 0.0, qa),
                                   jnp.where(lo_h, qb, 0.0), jnp.where(lo_h, 0.0, qb)], axis=0)
            s = lax.dot_general(lhs.astype(jnp.bfloat16), k2_ref[g, base:base + N_KEYS, :], _NT,
                                preferred_element_type=jnp.float32)
            s = jnp.where(valid, s, -jnp.inf)
            m = jnp.maximum(jnp.max(s, axis=-1, keepdims=True), sink)
            ex = jnp.exp2(s - m)
            den = jnp.sum(ex, axis=-1, keepdims=True) + jnp.exp2(sink - m)
            o4 = jnp.dot(ex.astype(jnp.bfloat16), v2_ref[g, base:base + N_KEYS, :],
                         preferred_element_type=jnp.float32) / den
            out_a.append(jnp.where(lo_h, o4[0:HALF], o4[HALF:2 * HALF]))
            out_b.append(jnp.where(lo_h, o4[2 * HALF:3 * HALF], o4[3 * HALF:]))
            yield
        put(2 * t, jnp.concatenate(out_a, axis=0) * z_ref[:, la])
        put(2 * t + 1, jnp.concatenate(out_b, axis=0) * z_ref[:, lb])


def _even_params(pre, win, qkvw, alog, dtb, onorm, dww, dwb, lnw, lnb, wout, post):
    return types.SimpleNamespace(pre=pre, win=win, qkvw=qkvw, alog=alog, dtb=dtb, onorm=onorm,
                                 dww=dww, dwb=dwb, lnw=lnw, lnb=lnb, wout=wout, post=post)


N_EVEN_PARAMS = 12


def _prologue_kernel(meta_ref, *refs):
    p = _even_params(*refs[:N_EVEN_PARAMS])
    (pre1, wkv1, s0_out, qkvt_out, ct_out, kmeta_out, vmeta_out,
     qkv_ext, c_ext, za_s, mixed_s) = refs[N_EVEN_PARAMS:]
    qkv_ext[0:QKV_HALO, :] = jnp.zeros((QKV_HALO, 3 * GDN_WIDTH), jnp.float32)
    c_ext[0:CONF_HALO, :] = jnp.zeros((CONF_HALO, CONF_WIDTH), jnp.float32)
    s0_out[...] = jnp.zeros(s0_out.shape, jnp.float32)

    h = meta_ref[...]
    proj = functools.partial(_RowProj([_rms(h, p.pre[...]).astype(jnp.bfloat16)], p.win), 0)

    def put_ob(val):
        mixed_s[:, GDN_WIDTH:] = val.astype(jnp.bfloat16)

    def put_oa(hd, val):
        mixed_s[:, hd * GDN_HEAD_DIM:(hd + 1) * GDN_HEAD_DIM] = val.astype(jnp.bfloat16)

    e = _even_front(proj, PAD_ROWS, p, qkv_ext)
    res = {}
    _interleave(_gdn_stages(e, s0_out, res), _conformer_stages(proj, p, c_ext, za_s, put_ob))
    _gdn_out(e, p, res, za_s, put_oa)
    y0 = jnp.dot(mixed_s[...], p.wout[...], preferred_element_type=jnp.float32)
    h1 = h + _rms(y0, p.post[...])
    qkvt_out[...] = qkv_ext[0:QKV_HALO, :]
    ct_out[...] = c_ext[0:CONF_HALO, :]

    hn1 = _rms(h1, pre1[...]).astype(jnp.bfloat16)
    k2, v2 = _dup_kv(jnp.dot(hn1, wkv1[...], preferred_element_type=jnp.float32))
    for g in range(SWA_KV_HEADS):
        kmeta_out[g] = k2[g][PAD_ROWS:, :].astype(jnp.bfloat16)
        vmeta_out[g] = v2[g][PAD_ROWS:, :].astype(jnp.bfloat16)


def _main_kernel(sinks_ref, x_ref, *refs):
    p = _even_params(*refs[:N_EVEN_PARAMS])
    (s0_ref, qkvt_ref, ct_ref, pre1, win1, wout1, post1, kmeta_ref, vmeta_ref,
     out_ref, qkv_ext, c_ext, s_ref, h1_s, k2_ref, v2_ref,
     q1_s, z1_s, za_s, mixed_s, gated_s) = refs[N_EVEN_PARAMS:]
    j = pl.program_id(1)
    seqs = range(SEQS)

    @pl.when(j == 0)
    def _():
        h1_s[...] = jnp.zeros(h1_s.shape, jnp.float32)
        for s in seqs:
            qkv_ext[s, 0:QKV_HALO, :] = qkvt_ref[...]
            c_ext[s, 0:CONF_HALO, :] = ct_ref[...]
            s_ref[s] = s0_ref[...]
            for g in range(SWA_KV_HEADS):
                for buf, meta in ((k2_ref, kmeta_ref), (v2_ref, vmeta_ref)):
                    buf[s, g, 0:N_META, :] = meta[g]
                    buf[s, g, KV_TAIL:KV_ROWS, :] = meta[g]
                    buf[s, g, KV_PREV:KV_CUR, :] = jnp.zeros((ROWS, LANES), jnp.bfloat16)

    h1 = [h1_s[s] for s in seqs]
    h = [x_ref[s] for s in seqs]
    proj1 = _RowProj([_rms(h1[s], pre1[...]).astype(jnp.bfloat16) for s in seqs], win1)
    proj0 = _RowProj([_rms(h[s], p.pre[...]).astype(jnp.bfloat16) for s in seqs], p.win)

    def put_ob(s, val):
        mixed_s[s * ROWS:(s + 1) * ROWS, GDN_WIDTH:] = val.astype(jnp.bfloat16)

    def put_oa(s, hd, val):
        mixed_s[s * ROWS:(s + 1) * ROWS, hd * GDN_HEAD_DIM:(hd + 1) * GDN_HEAD_DIM] = val.astype(jnp.bfloat16)

    def put_gated(s, pair, val):
        gated_s[s * ROWS:(s + 1) * ROWS, pair * LANES:(pair + 1) * LANES] = val.astype(jnp.bfloat16)

    e = [_even_front(functools.partial(proj0, s), 0, p, qkv_ext.at[s]) for s in seqs]
    res = [{} for _ in seqs]

    for s in seqs:
        k2, v2 = _dup_kv(proj1(s, SWA_Q_WIDTH, 2 * SWA_KV_WIDTH))
        for g in range(SWA_KV_HEADS):
            k2_ref[s, g, KV_CUR:KV_TAIL, :] = k2[g].astype(jnp.bfloat16)
            v2_ref[s, g, KV_CUR:KV_TAIL, :] = v2[g].astype(jnp.bfloat16)
    for s in seqs:
        q1_s[s] = proj1(s, 0, SWA_Q_WIDTH) * (SWA_HEAD_DIM ** -0.5 * LOG2_E)
    for s in seqs:
        z1_s[s] = _silu(proj1(s, SWA_Q_WIDTH + 2 * SWA_KV_WIDTH, SWA_Q_WIDTH))

    def gdn_stages(s):
        yield from _gdn_stages(e[s], s_ref.at[s], res[s])
        _gdn_out(e[s], p, res[s], za_s.at[s], functools.partial(put_oa, s))

    _interleave(*[gdn_stages(s) for s in seqs],
                *[_attention_stages(q1_s.at[s], j >= 2, sinks_ref, k2_ref.at[s], v2_ref.at[s],
                                    z1_s.at[s], functools.partial(put_gated, s)) for s in seqs],
                *[_conformer_stages(functools.partial(proj0, s), p, c_ext.at[s], za_s.at[s],
                                    functools.partial(put_ob, s)) for s in seqs])

    for s in seqs:
        for g in range(SWA_KV_HEADS):
            k2_ref[s, g, KV_PREV:KV_CUR, :] = k2_ref[s, g, KV_CUR:KV_TAIL, :]
            v2_ref[s, g, KV_PREV:KV_CUR, :] = v2_ref[s, g, KV_CUR:KV_TAIL, :]
    y1 = jnp.dot(gated_s[...], wout1[...], preferred_element_type=jnp.float32)
    for s in seqs:
        out_ref[s] = h1[s] + _rms(y1[s * ROWS:(s + 1) * ROWS], post1[...])

    y0 = jnp.dot(mixed_s[...], p.wout[...], preferred_element_type=jnp.float32)
    for s in seqs:
        h1_s[s] = h[s] + _rms(y0[s * ROWS:(s + 1) * ROWS], p.post[...])


def _lane_pad(v, offset):
    return jnp.zeros((1, LANES), jnp.float32).at[0, offset:offset + v.shape[0]].set(v)


def kernel(x, meta_tokens, even_pre_norm, even_w_in, even_qkv_conv, even_a_log, even_dt_bias,
           even_out_norm, even_dw_conv, even_dw_bias, even_ln_w, even_ln_b, even_w_out,
           even_post_norm, odd_pre_norm, odd_w_in, odd_sinks, odd_w_out, odd_post_norm):
    bsz, seq, d = x.shape
    assert d == D_MODEL and seq % ROWS == 0 and bsz % SEQS == 0
    nblk = seq // ROWS
    f32, bf16 = jnp.float32, jnp.bfloat16

    w0 = even_w_in[0]
    s_qkv, s_za, s_b, s_a, s_gv, s_gg, s_zb = jnp.split(
        w0, [1536, 2048, 2052, 2056, 2568, 3080], axis=1)
    gate_cols = jnp.concatenate(
        [s_b, s_a, jnp.zeros((d, LANES - 2 * GDN_HEADS), f32)], axis=1)
    w_in0 = jnp.concatenate([s_qkv, s_za, s_gv, s_gg, s_zb, gate_cols], axis=1).astype(bf16)
    w_out0 = even_w_out[0].astype(bf16)
    w_in1 = odd_w_in[0].astype(bf16)
    w_kv1 = w_in1[:, SWA_Q_WIDTH:SWA_Q_WIDTH + 2 * SWA_KV_WIDTH]
    w_out1 = odd_w_out[0].astype(bf16)
    meta_blk = jnp.concatenate([jnp.zeros((PAD_ROWS, d), f32), meta_tokens.astype(f32)], axis=0)
    row = lambda v: v.reshape(1, -1).astype(f32)
    even_args = (row(even_pre_norm[0]), w_in0, even_qkv_conv[0].astype(f32),
                 _lane_pad(even_a_log[0], GDN_HEADS), _lane_pad(even_dt_bias[0], GDN_HEADS),
                 row(even_out_norm[0]), even_dw_conv[0].astype(f32), row(even_dw_bias[0]),
                 row(even_ln_w[0]), row(even_ln_b[0]), w_out0, row(even_post_norm[0]))
    even_shapes = [a.shape for a in even_args]

    state_shapes = [
        jax.ShapeDtypeStruct((GDN_HEADS, GDN_HEAD_DIM, GDN_HEAD_DIM), f32),
        jax.ShapeDtypeStruct((QKV_HALO, 3 * GDN_WIDTH), f32),
        jax.ShapeDtypeStruct((CONF_HALO, CONF_WIDTH), f32),
        jax.ShapeDtypeStruct((SWA_KV_HEADS, N_META, LANES), bf16),
        jax.ShapeDtypeStruct((SWA_KV_HEADS, N_META, LANES), bf16),
    ]

    def full1(shape):
        return pl.BlockSpec(shape, lambda i: (0,) * len(shape))

    s0, qkv_tail, c_tail, k_meta, v_meta = pl.pallas_call(
        _prologue_kernel,
        grid=(1,),
        in_specs=[full1((ROWS, d))] + [full1(s) for s in even_shapes]
        + [full1((1, d)), full1((d, 2 * SWA_KV_WIDTH))],
        out_specs=[full1(s.shape) for s in state_shapes],
        out_shape=state_shapes,
        scratch_shapes=[pltpu.VMEM((QKV_HALO + ROWS, 3 * GDN_WIDTH), f32),
                        pltpu.VMEM((CONF_HALO + ROWS, CONF_WIDTH), f32),
                        pltpu.VMEM((ROWS, GDN_WIDTH), f32),
                        pltpu.VMEM((ROWS, d), bf16)],
        compiler_params=pltpu.CompilerParams(dimension_semantics=("arbitrary",),
                                             vmem_limit_bytes=VMEM_LIMIT_BYTES),
        name="meta_block",
    )(meta_blk, *even_args, row(odd_pre_norm[0]), w_kv1)

    def full2(shape):
        return pl.BlockSpec(shape, lambda b, j, s: (0,) * len(shape))

    out = pl.pallas_call(
        _main_kernel,
        grid_spec=pltpu.PrefetchScalarGridSpec(
            num_scalar_prefetch=1,
            grid=(bsz // SEQS, nblk + 1),
            in_specs=[pl.BlockSpec((SEQS, ROWS, d), lambda b, j, s: (b, jnp.minimum(j, nblk - 1), 0))]
            + [full2(s) for s in even_shapes]
            + [full2(s.shape) for s in state_shapes[:3]]
            + [full2((1, d)), full2((d, ODD_COLS)), full2((SWA_Q_WIDTH, d)), full2((1, d))]
            + [full2(s.shape) for s in state_shapes[3:]],
            out_specs=pl.BlockSpec((SEQS, ROWS, d), lambda b, j, s: (b, jnp.maximum(j - 1, 0), 0)),
            scratch_shapes=[
                pltpu.VMEM((SEQS, QKV_HALO + ROWS, 3 * GDN_WIDTH), f32),
                pltpu.VMEM((SEQS, CONF_HALO + ROWS, CONF_WIDTH), f32),
                pltpu.VMEM((SEQS, GDN_HEADS, GDN_HEAD_DIM, GDN_HEAD_DIM), f32),
                pltpu.VMEM((SEQS, ROWS, d), f32),
                pltpu.VMEM((SEQS, SWA_KV_HEADS, KV_ROWS, LANES), bf16),
                pltpu.VMEM((SEQS, SWA_KV_HEADS, KV_ROWS, LANES), bf16),
                pltpu.VMEM((SEQS, ROWS, SWA_Q_WIDTH), f32),
                pltpu.VMEM((SEQS, ROWS, SWA_Q_WIDTH), f32),
                pltpu.VMEM((SEQS, ROWS, GDN_WIDTH), f32),
                pltpu.VMEM((SEQS * ROWS, d), bf16),
                pltpu.VMEM((SEQS * ROWS, d), bf16),
            ],
        ),
        out_shape=jax.ShapeDtypeStruct((bsz, seq, d), f32),
        compiler_params=pltpu.CompilerParams(dimension_semantics=("parallel", "arbitrary"),
                                             vmem_limit_bytes=VMEM_LIMIT_BYTES),
        name="trunk",
    )(odd_sinks[0].astype(f32), x, *even_args, s0, qkv_tail, c_tail,
      row(odd_pre_norm[0]), w_in1, w_out1, row(odd_post_norm[0]), k_meta, v_meta)
    return out
```

```python
import functools
import types

import jax
import jax.numpy as jnp
from jax import lax
from jax.experimental import pallas as pl
from jax.experimental.pallas import tpu as pltpu

D_MODEL = 1024
N_META = 16
NORM_EPS = 1e-6
LOG2_E = 1.4426950408889634
NEG_LOG2_E = -LOG2_E

GDN_HEADS = 4
GDN_HEAD_DIM = 128
GDN_WIDTH = GDN_HEADS * GDN_HEAD_DIM
GDN_CONV = 4
GDN_CHUNK = 64
CONF_WIDTH = D_MODEL - GDN_WIDTH
CONF_CONV = 31

SWA_HEAD_DIM = 64
SWA_Q_HEADS = D_MODEL // SWA_HEAD_DIM
SWA_KV_HEADS = 2
SWA_WINDOW = 128
SWA_Q_WIDTH = SWA_Q_HEADS * SWA_HEAD_DIM
SWA_KV_WIDTH = SWA_KV_HEADS * SWA_HEAD_DIM
ODD_COLS = 2 * SWA_Q_WIDTH + 2 * SWA_KV_WIDTH

LANES = 128
SUBLANES = 8
ROWS = 128
HALF = ROWS // 2
SEQS = 2
PAD_ROWS = ROWS - N_META

C_QKV = 0
C_ZA = C_QKV + 3 * GDN_WIDTH
C_GV = C_ZA + GDN_WIDTH
C_GG = C_GV + CONF_WIDTH
C_ZB = C_GG + CONF_WIDTH
C_BA = C_ZB + CONF_WIDTH
EVEN_COLS = C_BA + LANES

QKV_HALO = 8
CONF_HALO = 32

KV_PREV, KV_CUR, KV_TAIL = N_META, N_META + ROWS, N_META + 2 * ROWS
KV_ROWS = KV_TAIL + N_META
N_WIN = 3 * HALF
N_KEYS = N_META + N_WIN

VMEM_LIMIT_BYTES = 56 * 1024 * 1024

_NT = (((1,), (1,)), ((), ()))


def _dot(a, b):
    return jnp.dot(a.astype(jnp.bfloat16), b.astype(jnp.bfloat16), preferred_element_type=jnp.float32)


def _dot_nt(a, b):
    return lax.dot_general(a.astype(jnp.bfloat16), b.astype(jnp.bfloat16), _NT,
                           preferred_element_type=jnp.float32)


def _rms(x, w):
    return x * lax.rsqrt(jnp.mean(x * x, axis=-1, keepdims=True) + NORM_EPS) * w


def _sigmoid(x):
    return 1.0 / (1.0 + jnp.exp2(x * NEG_LOG2_E))


def _silu(x):
    return x * _sigmoid(x)


def _softplus(x):
    return jnp.maximum(x, 0.0) + jnp.log(1.0 + jnp.exp(-jnp.abs(x)))


def _l2n(x):
    return x * lax.rsqrt(jnp.sum(x * x, axis=-1, keepdims=True) + NORM_EPS)


def _interleave(*stages):
    live = list(stages)
    while live:
        for g in list(live):
            try:
                next(g)
            except StopIteration:
                live.remove(g)


class _RowProj:
    def __init__(self, blocks, w_ref):
        self.lhs = blocks[0] if len(blocks) == 1 else jnp.concatenate(blocks, axis=0)
        self.w_ref = w_ref
        self.cache = {}

    def __call__(self, s, c0, width):
        if (c0, width) not in self.cache:
            self.cache[(c0, width)] = jnp.dot(self.lhs, self.w_ref[:, c0:c0 + width],
                                              preferred_element_type=jnp.float32)
        return self.cache[(c0, width)][s * ROWS:(s + 1) * ROWS]


def _gdn_stages(e, s_ref, res):
    n = ROWS
    q, k, v, beta, gc, glast, sd = e.q, e.k, e.v, e.beta, e.gc, e.glast, e.sd
    heads = range(len(q))
    row = lax.broadcasted_iota(jnp.int32, (n, n), 0)
    col = lax.broadcasted_iota(jnp.int32, (n, n), 1)
    same = (row >= GDN_CHUNK) == (col >= GDN_CHUNK)
    incl = same & (col <= row)
    strict = same & (col < row)
    eye = row == col
    ident = jnp.where(eye, 1.0, 0.0)

    kb = [k[h] * beta[h] for h in heads]
    kq = [_dot_nt(jnp.concatenate([kb[h], q[h]], axis=0), k[h]) for h in heads]
    yield

    gc_b = [jnp.broadcast_to(gc[h], (n, n)) for h in heads]
    gc_row = [jnp.sum(jnp.where(eye, gc_b[h], 0.0), axis=0, keepdims=True) for h in heads]
    decay = [jnp.exp(jnp.where(incl, gc_b[h] - gc_row[h], 0.0)) for h in heads]
    a = [kq[h][:n] * jnp.where(strict, decay[h], 0.0) for h in heads]
    qk = [kq[h][n:] * jnp.where(incl, decay[h], 0.0) for h in heads]

    p = [ident - a[h] for h in heads]
    bm = [_dot(a[h], a[h]) for h in heads]
    yield
    levels = GDN_CHUNK.bit_length() - 1
    for lvl in range(1, levels):
        if lvl < levels - 1:
            r = [_dot(bm[h], jnp.concatenate([bm[h], p[h]], axis=1)) for h in heads]
            bm = [r[h][:, :n] for h in heads]
            p = [p[h] + r[h][:, n:] for h in heads]
        else:
            p = [p[h] + _dot(bm[h], p[h]) for h in heads]
        yield

    egc = [jnp.exp(gc[h]) for h in heads]
    uw = [_dot(p[h], jnp.concatenate([v[h] * beta[h], kb[h] * egc[h]], axis=1)) for h in heads]
    qd = [q[h] * egc[h] for h in heads]
    kd_t = [jnp.transpose(k[h] * jnp.exp(glast[h] - gc[h])) for h in heads]
    yield

    outs = [[] for _ in heads]
    state = [s_ref[h] for h in heads]
    zeros = jnp.zeros((GDN_CHUNK, GDN_HEAD_DIM), jnp.float32)
    for c in range(n // GDN_CHUNK):
        sl = slice(c * GDN_CHUNK, (c + 1) * GDN_CHUNK)
        r1 = [_dot(jnp.concatenate([uw[h][sl, GDN_HEAD_DIM:], qd[h][sl]], axis=0), state[h])
              for h in heads]
        yield
        v_new = [uw[h][sl, :GDN_HEAD_DIM] - r1[h][:GDN_CHUNK] for h in heads]
        v_full = [jnp.concatenate([v_new[h], zeros] if c == 0 else [zeros, v_new[h]], axis=0)
                  for h in heads]
        r2 = [_dot(jnp.concatenate([qk[h][sl], kd_t[h]], axis=0), v_full[h]) for h in heads]
        for h in heads:
            outs[h].append(r1[h][GDN_CHUNK:] + r2[h][:GDN_CHUNK])
            state[h] = state[h] * sd[h][c] + r2[h][GDN_CHUNK:]
        yield
    for h in heads:
        s_ref[h] = state[h]
    res["o"] = [jnp.concatenate(o, axis=0) for o in outs]


def _shift_rows_up(x, k):
    n = x.shape[0] // SUBLANES
    keep = lax.broadcasted_iota(jnp.int32, (SUBLANES, x.shape[1]), 0) < SUBLANES - k
    rot = [pltpu.roll(x[i * SUBLANES:(i + 1) * SUBLANES], SUBLANES - k, axis=0) for i in range(n)]
    return jnp.concatenate([jnp.where(keep, rot[i], rot[(i + 1) % n]) for i in range(n)], axis=0)


def _causal_conv_stages(ext_ref, w_ref, taps, halo, lanes, res, bias_ref=None):
    ext = ext_ref[:, lanes]
    n = ext.shape[0]
    first = halo - (taps - 1)
    acc = None if bias_ref is None else bias_ref[:, lanes]
    groups = 0
    shifted, shifted_by = ext, 0
    for phase in range(SUBLANES):
        starts = [first + t for t in range(taps) if (first + t) % SUBLANES == phase]
        if not starts:
            continue
        if phase:
            shifted, shifted_by = _shift_rows_up(shifted, phase - shifted_by), phase
        for st in starts:
            t = st - first
            term = w_ref[t:t + 1, lanes] * shifted[st - phase:st - phase + ROWS, :]
            acc = term if acc is None else acc + term
        groups += 1
        if groups % 2 == 0:
            yield
    res["conv"] = acc


def _run(gen):
    for _ in gen:
        pass


def _even_front(proj, pad_rows, p, qkv_ext):
    e = types.SimpleNamespace()
    qkv_ext[QKV_HALO:QKV_HALO + ROWS, :] = proj(C_QKV, 3 * GDN_WIDTH)
    res = {}
    qkv = [None] * (3 * GDN_HEADS)
    for c in [part * GDN_HEADS + hd for hd in range(GDN_HEADS) for part in range(3)]:
        lanes = slice(c * LANES, (c + 1) * LANES)
        _run(_causal_conv_stages(qkv_ext, p.qkvw, GDN_CONV, QKV_HALO, lanes, res))
        qkv[c] = _silu(res["conv"])
    qkv_ext[0:QKV_HALO, :] = qkv_ext[ROWS:ROWS + QKV_HALO, :]

    gate = proj(C_BA, LANES)
    rowi = lax.broadcasted_iota(jnp.int32, (ROWS, LANES), 0)
    beta_all = _sigmoid(gate)
    g_all = -jnp.exp(p.alog[...]) * _softplus(gate + p.dtb[...])
    if pad_rows:
        g_all = jnp.where(rowi >= pad_rows, g_all, 0.0)
    r2 = lax.broadcasted_iota(jnp.int32, (ROWS, ROWS), 0)
    c2 = lax.broadcasted_iota(jnp.int32, (ROWS, ROWS), 1)
    tri = jnp.where(((r2 >= GDN_CHUNK) == (c2 >= GDN_CHUNK)) & (c2 <= r2), 1.0, 0.0)
    gc_all = jnp.dot(tri, g_all, preferred_element_type=jnp.float32,
                     precision=lax.Precision.HIGHEST)

    top = rowi[:, 0:1] < GDN_CHUNK
    hs = range(GDN_HEADS)
    e.hsl = [slice(hd * GDN_HEAD_DIM, (hd + 1) * GDN_HEAD_DIM) for hd in hs]
    e.q = [_l2n(qkv[hd]) * (GDN_HEAD_DIM ** -0.5) for hd in hs]
    e.k = [_l2n(qkv[GDN_HEADS + hd]) for hd in hs]
    e.v = [qkv[2 * GDN_HEADS + hd] for hd in hs]
    e.beta = [beta_all[:, hd:hd + 1] for hd in hs]
    e.gc = [gc_all[:, GDN_HEADS + hd:GDN_HEADS + hd + 1] for hd in hs]
    gl0 = [e.gc[hd][GDN_CHUNK - 1:GDN_CHUNK, :] for hd in hs]
    gl1 = [e.gc[hd][ROWS - 1:ROWS, :] for hd in hs]
    e.glast = [jnp.where(top, gl0[hd], gl1[hd]) for hd in hs]
    e.sd = [(jnp.exp(gl0[hd]), jnp.exp(gl1[hd])) for hd in hs]
    return e


def _conformer_stages(proj, p, c_ext, za_ref, put_ob):
    c_ext[CONF_HALO:CONF_HALO + ROWS, :] = proj(C_GV, CONF_WIDTH) * _sigmoid(proj(C_GG, CONF_WIDTH))
    yield
    res = {}
    zb = None
    chunks = []
    for c in range(CONF_WIDTH // LANES):
        lanes = slice(c * LANES, (c + 1) * LANES)
        yield from _causal_conv_stages(c_ext, p.dww, CONF_CONV, CONF_HALO, lanes, res, p.dwb)
        chunks.append(res["conv"])
        if c == 0:
            za_ref[...] = _silu(proj(C_ZA, GDN_WIDTH))
        if c == 2:
            zb = _silu(proj(C_ZB, CONF_WIDTH))
        yield
    acc = jnp.concatenate(chunks, axis=1)
    c_ext[0:CONF_HALO, :] = c_ext[ROWS:ROWS + CONF_HALO, :]
    mu = jnp.mean(acc, axis=-1, keepdims=True)
    cen = acc - mu
    var = jnp.mean(cen * cen, axis=-1, keepdims=True)
    ln = cen * lax.rsqrt(var + NORM_EPS) * p.lnw[...] + p.lnb[...]
    put_ob(_silu(ln) * zb)


def _gdn_out(e, p, res, za_ref, put_oa):
    for hd in range(GDN_HEADS):
        put_oa(hd, _rms(res["o"][hd], p.onorm[...]) * za_ref[:, e.hsl[hd]])


def _dup_kv(kv):
    lo = lax.broadcasted_iota(jnp.int32, (ROWS, LANES), 1) < SWA_HEAD_DIM
    kk, vv = kv[:, :LANES], kv[:, LANES:]
    kr = pltpu.roll(kk, SWA_HEAD_DIM, axis=1)
    vr = pltpu.roll(vv, SWA_HEAD_DIM, axis=1)
    k2 = (jnp.where(lo, kk, kr), jnp.where(lo, kr, kk))
    v2 = (jnp.where(lo, vv, vr), jnp.where(lo, vr, vv))
    return k2, v2


def _attention_stages(q_ref, has_prev, sinks_ref, k2_ref, v2_ref, z_ref, put):
    quad = 4
    lo_h = lax.broadcasted_iota(jnp.int32, (HALF, LANES), 1) < SWA_HEAD_DIM
    groups = []
    for u in range(ROWS // HALF):
        base = 0 if u == 0 else KV_PREV + HALF
        r = (lax.broadcasted_iota(jnp.int32, (quad * HALF, N_KEYS), 0) & (HALF - 1)) + u * HALF
        c = lax.broadcasted_iota(jnp.int32, (quad * HALF, N_KEYS), 1)
        if u == 0:
            is_meta, kp = c < N_META, c - N_META - ROWS
        else:
            is_meta, kp = c >= N_WIN, c - HALF
        valid = is_meta | ((kp <= r) & (kp > r - SWA_WINDOW) & ((kp >= 0) | has_prev))
        groups.append((slice(u * HALF, (u + 1) * HALF), base, valid))
    grp = lax.broadcasted_iota(jnp.int32, (quad * HALF, 1), 0) // HALF
    for t in range(SWA_Q_HEADS // quad):
        g = (quad * t) // (SWA_Q_HEADS // SWA_KV_HEADS)
        la = slice((2 * t) * LANES, (2 * t + 1) * LANES)
        lb = slice((2 * t + 1) * LANES, (2 * t + 2) * LANES)
        sink = LOG2_E * jnp.where(grp == 0, sinks_ref[quad * t],
                                  jnp.where(grp == 1, sinks_ref[quad * t + 1],
                                            jnp.where(grp == 2, sinks_ref[quad * t + 2],
                                                      sinks_ref[quad * t + 3])))
        out_a, out_b = [], []
        for rows, base, valid in groups:
            qa, qb = q_ref[rows, la], q_ref[rows, lb]
            lhs = jnp.concatenate([jnp.where(lo_h, qa, 0.0), jnp.where(lo_h, 0.0, qa),
                                   jnp.where(lo_h, qb, 0.0), jnp.where(lo_h, 0.0, qb)], axis=0)
            s = lax.dot_general(lhs.astype(jnp.bfloat16), k2_ref[g, base:base + N_KEYS, :], _NT,
                                preferred_element_type=jnp.float32)
            s = jnp.where(valid, s, -jnp.inf)
            m = jnp.maximum(jnp.max(s, axis=-1, keepdims=True), sink)
            ex = jnp.exp2(s - m)
            den = jnp.sum(ex, axis=-1, keepdims=True) + jnp.exp2(sink - m)
            o4 = jnp.dot(ex.astype(jnp.bfloat16), v2_ref[g, base:base + N_KEYS, :],
                         preferred_element_type=jnp.float32) / den
            out_a.append(jnp.where(lo_h, o4[0:HALF], o4[HALF:2 * HALF]))
            out_b.append(jnp.where(lo_h, o4[2 * HALF:3 * HALF], o4[3 * HALF:]))
            yield
        put(2 * t, jnp.concatenate(out_a, axis=0) * z_ref[:, la])
        put(2 * t + 1, jnp.concatenate(out_b, axis=0) * z_ref[:, lb])


def _even_params(pre, win, qkvw, alog, dtb, onorm, dww, dwb, lnw, lnb, wout, post):
    return types.SimpleNamespace(pre=pre, win=win, qkvw=qkvw, alog=alog, dtb=dtb, onorm=onorm,
                                 dww=dww, dwb=dwb, lnw=lnw, lnb=lnb, wout=wout, post=post)


N_EVEN_PARAMS = 12


def _prologue_kernel(meta_ref, *refs):
    p = _even_params(*refs[:N_EVEN_PARAMS])
    (pre1, wkv1, s0_out, qkvt_out, ct_out, kmeta_out, vmeta_out,
     qkv_ext, c_ext, za_s, mixed_s) = refs[N_EVEN_PARAMS:]
    qkv_ext[0:QKV_HALO, :] = jnp.zeros((QKV_HALO, 3 * GDN_WIDTH), jnp.float32)
    c_ext[0:CONF_HALO, :] = jnp.zeros((CONF_HALO, CONF_WIDTH), jnp.float32)
    s0_out[...] = jnp.zeros(s0_out.shape, jnp.float32)

    h = meta_ref[...]
    proj = functools.partial(_RowProj([_rms(h, p.pre[...]).astype(jnp.bfloat16)], p.win), 0)

    def put_ob(val):
        mixed_s[:, GDN_WIDTH:] = val.astype(jnp.bfloat16)

    def put_oa(hd, val):
        mixed_s[:, hd * GDN_HEAD_DIM:(hd + 1) * GDN_HEAD_DIM] = val.astype(jnp.bfloat16)

    e = _even_front(proj, PAD_ROWS, p, qkv_ext)
    res = {}
    _interleave(_gdn_stages(e, s0_out, res), _conformer_stages(proj, p, c_ext, za_s, put_ob))
    _gdn_out(e, p, res, za_s, put_oa)
    y0 = jnp.dot(mixed_s[...], p.wout[...], preferred_element_type=jnp.float32)
    h1 = h + _rms(y0, p.post[...])
    qkvt_out[...] = qkv_ext[0:QKV_HALO, :]
    ct_out[...] = c_ext[0:CONF_HALO, :]

    hn1 = _rms(h1, pre1[...]).astype(jnp.bfloat16)
    k2, v2 = _dup_kv(jnp.dot(hn1, wkv1[...], preferred_element_type=jnp.float32))
    for g in range(SWA_KV_HEADS):
        kmeta_out[g] = k2[g][PAD_ROWS:, :].astype(jnp.bfloat16)
        vmeta_out[g] = v2[g][PAD_ROWS:, :].astype(jnp.bfloat16)


def _main_kernel(nblk, sinks_ref, x_ref, *refs):
    p = _even_params(*refs[:N_EVEN_PARAMS])
    (s0_ref, qkvt_ref, ct_ref, pre1, win1, wout1, post1, kmeta_ref, vmeta_ref,
     out_ref, qkv_ext, c_ext, s_ref, h1_s, k2_ref, v2_ref,
     q1_s, z1_s, za_s, mixed_s, gated_s) = refs[N_EVEN_PARAMS:]
    j = pl.program_id(1)
    seqs = range(SEQS)

    @pl.when(j == 0)
    def _():
        for s in seqs:
            qkv_ext[s, 0:QKV_HALO, :] = qkvt_ref[...]
            c_ext[s, 0:CONF_HALO, :] = ct_ref[...]
            s_ref[s] = s0_ref[...]
            for g in range(SWA_KV_HEADS):
                for buf, meta in ((k2_ref, kmeta_ref), (v2_ref, vmeta_ref)):
                    buf[s, g, 0:N_META, :] = meta[g]
                    buf[s, g, KV_TAIL:KV_ROWS, :] = meta[g]
                    buf[s, g, KV_PREV:KV_CUR, :] = jnp.zeros((ROWS, LANES), jnp.bfloat16)

    def put_ob(s, val):
        mixed_s[s * ROWS:(s + 1) * ROWS, GDN_WIDTH:] = val.astype(jnp.bfloat16)

    def put_oa(s, hd, val):
        mixed_s[s * ROWS:(s + 1) * ROWS, hd * GDN_HEAD_DIM:(hd + 1) * GDN_HEAD_DIM] = val.astype(jnp.bfloat16)

    def put_gated(s, pair, val):
        gated_s[s * ROWS:(s + 1) * ROWS, pair * LANES:(pair + 1) * LANES] = val.astype(jnp.bfloat16)

    def step(do_even, do_odd):
        stages = []
        if do_even:
            h = [x_ref[s] for s in seqs]
            proj0 = _RowProj([_rms(h[s], p.pre[...]).astype(jnp.bfloat16) for s in seqs], p.win)
            e = [_even_front(functools.partial(proj0, s), 0, p, qkv_ext.at[s]) for s in seqs]
            res = [{} for _ in seqs]

            def gdn_stages(s):
                yield from _gdn_stages(e[s], s_ref.at[s], res[s])
                _gdn_out(e[s], p, res[s], za_s.at[s], functools.partial(put_oa, s))

            stages += [gdn_stages(s) for s in seqs]
        if do_odd:
            h1 = [h1_s[s] for s in seqs]
            proj1 = _RowProj([_rms(h1[s], pre1[...]).astype(jnp.bfloat16) for s in seqs], win1)
            for s in seqs:
                k2, v2 = _dup_kv(proj1(s, SWA_Q_WIDTH, 2 * SWA_KV_WIDTH))
                for g in range(SWA_KV_HEADS):
                    k2_ref[s, g, KV_CUR:KV_TAIL, :] = k2[g].astype(jnp.bfloat16)
                    v2_ref[s, g, KV_CUR:KV_TAIL, :] = v2[g].astype(jnp.bfloat16)
            for s in seqs:
                q1_s[s] = proj1(s, 0, SWA_Q_WIDTH) * (SWA_HEAD_DIM ** -0.5 * LOG2_E)
            for s in seqs:
                z1_s[s] = _silu(proj1(s, SWA_Q_WIDTH + 2 * SWA_KV_WIDTH, SWA_Q_WIDTH))
            stages += [_attention_stages(q1_s.at[s], j >= 2, sinks_ref, k2_ref.at[s], v2_ref.at[s],
                                         z1_s.at[s], functools.partial(put_gated, s)) for s in seqs]
        if do_even:
            stages += [_conformer_stages(functools.partial(proj0, s), p, c_ext.at[s], za_s.at[s],
                                         functools.partial(put_ob, s)) for s in seqs]
        _interleave(*stages)

        if do_odd:
            for s in seqs:
                for g in range(SWA_KV_HEADS):
                    k2_ref[s, g, KV_PREV:KV_CUR, :] = k2_ref[s, g, KV_CUR:KV_TAIL, :]
                    v2_ref[s, g, KV_PREV:KV_CUR, :] = v2_ref[s, g, KV_CUR:KV_TAIL, :]
            y1 = jnp.dot(gated_s[...], wout1[...], preferred_element_type=jnp.float32)
            for s in seqs:
                out_ref[s] = h1[s] + _rms(y1[s * ROWS:(s + 1) * ROWS], post1[...])
        if do_even:
            y0 = jnp.dot(mixed_s[...], p.wout[...], preferred_element_type=jnp.float32)
            for s in seqs:
                h1_s[s] = h[s] + _rms(y0[s * ROWS:(s + 1) * ROWS], p.post[...])

    pl.when(j == 0)(functools.partial(step, True, False))
    pl.when((j > 0) & (j < nblk))(functools.partial(step, True, True))
    pl.when(j == nblk)(functools.partial(step, False, True))


def _lane_pad(v, offset):
    return jnp.zeros((1, LANES), jnp.float32).at[0, offset:offset + v.shape[0]].set(v)


def kernel(x, meta_tokens, even_pre_norm, even_w_in, even_qkv_conv, even_a_log, even_dt_bias,
           even_out_norm, even_dw_conv, even_dw_bias, even_ln_w, even_ln_b, even_w_out,
           even_post_norm, odd_pre_norm, odd_w_in, odd_sinks, odd_w_out, odd_post_norm):
    bsz, seq, d = x.shape
    assert d == D_MODEL and seq % ROWS == 0 and bsz % SEQS == 0
    nblk = seq // ROWS
    f32, bf16 = jnp.float32, jnp.bfloat16

    w0 = even_w_in[0]
    s_qkv, s_za, s_b, s_a, s_gv, s_gg, s_zb = jnp.split(
        w0, [1536, 2048, 2052, 2056, 2568, 3080], axis=1)
    gate_cols = jnp.concatenate(
        [s_b, s_a, jnp.zeros((d, LANES - 2 * GDN_HEADS), f32)], axis=1)
    w_in0 = jnp.concatenate([s_qkv, s_za, s_gv, s_gg, s_zb, gate_cols], axis=1).astype(bf16)
    w_out0 = even_w_out[0].astype(bf16)
    w_in1 = odd_w_in[0].astype(bf16)
    w_kv1 = w_in1[:, SWA_Q_WIDTH:SWA_Q_WIDTH + 2 * SWA_KV_WIDTH]
    w_out1 = odd_w_out[0].astype(bf16)
    meta_blk = jnp.concatenate([jnp.zeros((PAD_ROWS, d), f32), meta_tokens.astype(f32)], axis=0)
    row = lambda v: v.reshape(1, -1).astype(f32)
    even_args = (row(even_pre_norm[0]), w_in0, even_qkv_conv[0].astype(f32),
                 _lane_pad(even_a_log[0], GDN_HEADS), _lane_pad(even_dt_bias[0], GDN_HEADS),
                 row(even_out_norm[0]), even_dw_conv[0].astype(f32), row(even_dw_bias[0]),
                 row(even_ln_w[0]), row(even_ln_b[0]), w_out0, row(even_post_norm[0]))
    even_shapes = [a.shape for a in even_args]

    state_shapes = [
        jax.ShapeDtypeStruct((GDN_HEADS, GDN_HEAD_DIM, GDN_HEAD_DIM), f32),
        jax.ShapeDtypeStruct((QKV_HALO, 3 * GDN_WIDTH), f32),
        jax.ShapeDtypeStruct((CONF_HALO, CONF_WIDTH), f32),
        jax.ShapeDtypeStruct((SWA_KV_HEADS, N_META, LANES), bf16),
        jax.ShapeDtypeStruct((SWA_KV_HEADS, N_META, LANES), bf16),
    ]

    def full1(shape):
        return pl.BlockSpec(shape, lambda i: (0,) * len(shape))

    s0, qkv_tail, c_tail, k_meta, v_meta = pl.pallas_call(
        _prologue_kernel,
        grid=(1,),
        in_specs=[full1((ROWS, d))] + [full1(s) for s in even_shapes]
        + [full1((1, d)), full1((d, 2 * SWA_KV_WIDTH))],
        out_specs=[full1(s.shape) for s in state_shapes],
        out_shape=state_shapes,
        scratch_shapes=[pltpu.VMEM((QKV_HALO + ROWS, 3 * GDN_WIDTH), f32),
                        pltpu.VMEM((CONF_HALO + ROWS, CONF_WIDTH), f32),
                        pltpu.VMEM((ROWS, GDN_WIDTH), f32),
                        pltpu.VMEM((ROWS, d), bf16)],
        compiler_params=pltpu.CompilerParams(dimension_semantics=("arbitrary",),
                                             vmem_limit_bytes=VMEM_LIMIT_BYTES),
        name="meta_block",
    )(meta_blk, *even_args, row(odd_pre_norm[0]), w_kv1)

    def full2(shape):
        return pl.BlockSpec(shape, lambda b, j, s: (0,) * len(shape))

    out = pl.pallas_call(
        functools.partial(_main_kernel, nblk),
        grid_spec=pltpu.PrefetchScalarGridSpec(
            num_scalar_prefetch=1,
            grid=(bsz // SEQS, nblk + 1),
            in_specs=[pl.BlockSpec((SEQS, ROWS, d), lambda b, j, s: (b, jnp.minimum(j, nblk - 1), 0))]
            + [full2(s) for s in even_shapes]
            + [full2(s.shape) for s in state_shapes[:3]]
            + [full2((1, d)), full2((d, ODD_COLS)), full2((SWA_Q_WIDTH, d)), full2((1, d))]
            + [full2(s.shape) for s in state_shapes[3:]],
            out_specs=pl.BlockSpec((SEQS, ROWS, d), lambda b, j, s: (b, jnp.maximum(j - 1, 0), 0)),
            scratch_shapes=[
                pltpu.VMEM((SEQS, QKV_HALO + ROWS, 3 * GDN_WIDTH), f32),
                pltpu.VMEM((SEQS, CONF_HALO + ROWS, CONF_WIDTH), f32),
                pltpu.VMEM((SEQS, GDN_HEADS, GDN_HEAD_DIM, GDN_HEAD_DIM), f32),
                pltpu.VMEM((SEQS, ROWS, d), f32),
                pltpu.VMEM((SEQS, SWA_KV_HEADS, KV_ROWS, LANES), bf16),
                pltpu.VMEM((SEQS, SWA_KV_HEADS, KV_ROWS, LANES), bf16),
                pltpu.VMEM((SEQS, ROWS, SWA_Q_WIDTH), f32),
                pltpu.VMEM((SEQS, ROWS, SWA_Q_WIDTH), f32),
                pltpu.VMEM((SEQS, ROWS, GDN_WIDTH), f32),
                pltpu.VMEM((SEQS * ROWS, d), bf16),
                pltpu.VMEM((SEQS * ROWS, d), bf16),
            ],
        ),
        out_shape=jax.ShapeDtypeStruct((bsz, seq, d), f32),
        compiler_params=pltpu.CompilerParams(dimension_semantics=("parallel", "arbitrary"),
                                             vmem_limit_bytes=VMEM_LIMIT_BYTES),
        name="trunk",
    )(odd_sinks[0].astype(f32), x, *even_args, s0, qkv_tail, c_tail,
      row(odd_pre_norm[0]), w_in1, w_out1, row(odd_post_norm[0]), k_meta, v_meta)
    return out
```

```python
import functools
import types

import jax
import jax.numpy as jnp
from jax import lax
from jax.experimental import pallas as pl
from jax.experimental.pallas import tpu as pltpu

D_MODEL = 1024
N_META = 16
NORM_EPS = 1e-6
LOG2_E = 1.4426950408889634
NEG_LOG2_E = -LOG2_E

GDN_HEADS = 4
GDN_HEAD_DIM = 128
GDN_WIDTH = GDN_HEADS * GDN_HEAD_DIM
GDN_CONV = 4
GDN_CHUNK = 64
CONF_WIDTH = D_MODEL - GDN_WIDTH
CONF_CONV = 31

SWA_HEAD_DIM = 64
SWA_Q_HEADS = D_MODEL // SWA_HEAD_DIM
SWA_KV_HEADS = 2
SWA_WINDOW = 128
SWA_Q_WIDTH = SWA_Q_HEADS * SWA_HEAD_DIM
SWA_KV_WIDTH = SWA_KV_HEADS * SWA_HEAD_DIM
ODD_COLS = 2 * SWA_Q_WIDTH + 2 * SWA_KV_WIDTH

LANES = 128
SUBLANES = 8
ROWS = 128
HALF = ROWS // 2
SEQS = 2
PAD_ROWS = ROWS - N_META

C_QKV = 0
C_ZA = C_QKV + 3 * GDN_WIDTH
C_GV = C_ZA + GDN_WIDTH
C_GG = C_GV + CONF_WIDTH
C_ZB = C_GG + CONF_WIDTH
C_BA = C_ZB + CONF_WIDTH
EVEN_COLS = C_BA + LANES

QKV_HALO = 8
CONF_HALO = 32

KV_PREV, KV_CUR, KV_TAIL = N_META, N_META + ROWS, N_META + 2 * ROWS
KV_ROWS = KV_TAIL + N_META
N_WIN = 3 * HALF
N_KEYS = N_META + N_WIN

VMEM_LIMIT_BYTES = 56 * 1024 * 1024

_NT = (((1,), (1,)), ((), ()))


def _dot(a, b):
    return jnp.dot(a.astype(jnp.bfloat16), b.astype(jnp.bfloat16), preferred_element_type=jnp.float32)


def _dot_nt(a, b):
    return lax.dot_general(a.astype(jnp.bfloat16), b.astype(jnp.bfloat16), _NT,
                           preferred_element_type=jnp.float32)


def _rms(x, w):
    return x * lax.rsqrt(jnp.mean(x * x, axis=-1, keepdims=True) + NORM_EPS) * w


def _sigmoid(x):
    return 1.0 / (1.0 + jnp.exp2(x * NEG_LOG2_E))


def _silu(x):
    return x * _sigmoid(x)


def _softplus(x):
    return jnp.maximum(x, 0.0) + jnp.log(1.0 + jnp.exp(-jnp.abs(x)))


def _l2n(x):
    return x * lax.rsqrt(jnp.sum(x * x, axis=-1, keepdims=True) + NORM_EPS)


def _interleave(*stages):
    live = list(stages)
    while live:
        for g in list(live):
            try:
                next(g)
            except StopIteration:
                live.remove(g)


class _RowProj:
    def __init__(self, blocks, w_ref):
        self.lhs = blocks[0] if len(blocks) == 1 else jnp.concatenate(blocks, axis=0)
        self.w_ref = w_ref
        self.cache = {}

    def __call__(self, s, c0, width):
        if (c0, width) not in self.cache:
            self.cache[(c0, width)] = jnp.dot(self.lhs, self.w_ref[:, c0:c0 + width],
                                              preferred_element_type=jnp.float32)
        return self.cache[(c0, width)][s * ROWS:(s + 1) * ROWS]


def _gdn_stages(e, s_ref, res):
    n = ROWS
    q, k, v, beta, gc, glast, sd = e.q, e.k, e.v, e.beta, e.gc, e.glast, e.sd
    heads = range(len(q))
    row = lax.broadcasted_iota(jnp.int32, (n, n), 0)
    col = lax.broadcasted_iota(jnp.int32, (n, n), 1)
    same = (row >= GDN_CHUNK) == (col >= GDN_CHUNK)
    incl = same & (col <= row)
    strict = same & (col < row)
    eye = row == col
    ident = jnp.where(eye, 1.0, 0.0)

    kb = [k[h] * beta[h] for h in heads]
    kq = [_dot_nt(jnp.concatenate([kb[h], q[h]], axis=0), k[h]) for h in heads]
    yield

    gc_b = [jnp.broadcast_to(gc[h], (n, n)) for h in heads]
    gc_row = [jnp.sum(jnp.where(eye, gc_b[h], 0.0), axis=0, keepdims=True) for h in heads]
    decay = [jnp.exp(jnp.where(incl, gc_b[h] - gc_row[h], 0.0)) for h in heads]
    a = [kq[h][:n] * jnp.where(strict, decay[h], 0.0) for h in heads]
    qk = [kq[h][n:] * jnp.where(incl, decay[h], 0.0) for h in heads]

    p = [ident - a[h] for h in heads]
    bm = [_dot(a[h], a[h]) for h in heads]
    yield
    levels = GDN_CHUNK.bit_length() - 1
    for lvl in range(1, levels):
        if lvl < levels - 1:
            r = [_dot(bm[h], jnp.concatenate([bm[h], p[h]], axis=1)) for h in heads]
            bm = [r[h][:, :n] for h in heads]
            p = [p[h] + r[h][:, n:] for h in heads]
        else:
            p = [p[h] + _dot(bm[h], p[h]) for h in heads]
        yield

    egc = [jnp.exp(gc[h]) for h in heads]
    uw = [_dot(p[h], jnp.concatenate([v[h] * beta[h], kb[h] * egc[h]], axis=1)) for h in heads]
    qd = [q[h] * egc[h] for h in heads]
    kd_t = [jnp.transpose(k[h] * jnp.exp(glast[h] - gc[h])) for h in heads]
    yield

    outs = [[] for _ in heads]
    state = [s_ref[h] for h in heads]
    zeros = jnp.zeros((GDN_CHUNK, GDN_HEAD_DIM), jnp.float32)
    for c in range(n // GDN_CHUNK):
        sl = slice(c * GDN_CHUNK, (c + 1) * GDN_CHUNK)
        r1 = [_dot(jnp.concatenate([uw[h][sl, GDN_HEAD_DIM:], qd[h][sl]], axis=0), state[h])
              for h in heads]
        yield
        v_new = [uw[h][sl, :GDN_HEAD_DIM] - r1[h][:GDN_CHUNK] for h in heads]
        v_full = [jnp.concatenate([v_new[h], zeros] if c == 0 else [zeros, v_new[h]], axis=0)
                  for h in heads]
        r2 = [_dot(jnp.concatenate([qk[h][sl], kd_t[h]], axis=0), v_full[h]) for h in heads]
        for h in heads:
            outs[h].append(r1[h][GDN_CHUNK:] + r2[h][:GDN_CHUNK])
            state[h] = state[h] * sd[h][c] + r2[h][GDN_CHUNK:]
        yield
    for h in heads:
        s_ref[h] = state[h]
    res["o"] = [jnp.concatenate(o, axis=0) for o in outs]


def _shift_rows_up(x, k):
    n = x.shape[0] // SUBLANES
    keep = lax.broadcasted_iota(jnp.int32, (SUBLANES, x.shape[1]), 0) < SUBLANES - k
    rot = [pltpu.roll(x[i * SUBLANES:(i + 1) * SUBLANES], SUBLANES - k, axis=0) for i in range(n)]
    return jnp.concatenate([jnp.where(keep, rot[i], rot[(i + 1) % n]) for i in range(n)], axis=0)


def _causal_conv_stages(ext_ref, w_ref, taps, halo, lanes, res, bias_ref=None):
    ext = ext_ref[:, lanes]
    n = ext.shape[0]
    first = halo - (taps - 1)
    acc = None if bias_ref is None else bias_ref[:, lanes]
    groups = 0
    shifted, shifted_by = ext, 0
    for phase in range(SUBLANES):
        starts = [first + t for t in range(taps) if (first + t) % SUBLANES == phase]
        if not starts:
            continue
        if phase:
            shifted, shifted_by = _shift_rows_up(shifted, phase - shifted_by), phase
        for st in starts:
            t = st - first
            term = w_ref[t:t + 1, lanes] * shifted[st - phase:st - phase + ROWS, :]
            acc = term if acc is None else acc + term
        groups += 1
        if groups % 2 == 0:
            yield
    res["conv"] = acc


def _run(gen):
    for _ in gen:
        pass


def _even_front(proj, pad_rows, p, qkv_ext):
    e = types.SimpleNamespace()
    qkv_ext[QKV_HALO:QKV_HALO + ROWS, :] = proj(C_QKV, 3 * GDN_WIDTH)
    res = {}
    qkv = [None] * (3 * GDN_HEADS)
    for c in [part * GDN_HEADS + hd for hd in range(GDN_HEADS) for part in range(3)]:
        lanes = slice(c * LANES, (c + 1) * LANES)
        _run(_causal_conv_stages(qkv_ext, p.qkvw, GDN_CONV, QKV_HALO, lanes, res))
        qkv[c] = _silu(res["conv"])
    qkv_ext[0:QKV_HALO, :] = qkv_ext[ROWS:ROWS + QKV_HALO, :]

    gate = proj(C_BA, LANES)
    rowi = lax.broadcasted_iota(jnp.int32, (ROWS, LANES), 0)
    beta_all = _sigmoid(gate)
    g_all = -jnp.exp(p.alog[...]) * _softplus(gate + p.dtb[...])
    if pad_rows:
        g_all = jnp.where(rowi >= pad_rows, g_all, 0.0)
    r2 = lax.broadcasted_iota(jnp.int32, (ROWS, ROWS), 0)
    c2 = lax.broadcasted_iota(jnp.int32, (ROWS, ROWS), 1)
    tri = jnp.where(((r2 >= GDN_CHUNK) == (c2 >= GDN_CHUNK)) & (c2 <= r2), 1.0, 0.0)
    gc_all = jnp.dot(tri, g_all, preferred_element_type=jnp.float32,
                     precision=lax.Precision.HIGHEST)

    top = rowi[:, 0:1] < GDN_CHUNK
    hs = range(GDN_HEADS)
    e.hsl = [slice(hd * GDN_HEAD_DIM, (hd + 1) * GDN_HEAD_DIM) for hd in hs]
    e.q = [_l2n(qkv[hd]) * (GDN_HEAD_DIM ** -0.5) for hd in hs]
    e.k = [_l2n(qkv[GDN_HEADS + hd]) for hd in hs]
    e.v = [qkv[2 * GDN_HEADS + hd] for hd in hs]
    e.beta = [beta_all[:, hd:hd + 1] for hd in hs]
    e.gc = [gc_all[:, GDN_HEADS + hd:GDN_HEADS + hd + 1] for hd in hs]
    gl0 = [e.gc[hd][GDN_CHUNK - 1:GDN_CHUNK, :] for hd in hs]
    gl1 = [e.gc[hd][ROWS - 1:ROWS, :] for hd in hs]
    e.glast = [jnp.where(top, gl0[hd], gl1[hd]) for hd in hs]
    e.sd = [(jnp.exp(gl0[hd]), jnp.exp(gl1[hd])) for hd in hs]
    return e


def _conformer_stages(proj, p, c_ext, za_ref, put_ob):
    c_ext[CONF_HALO:CONF_HALO + ROWS, :] = proj(C_GV, CONF_WIDTH) * _sigmoid(proj(C_GG, CONF_WIDTH))
    yield
    res = {}
    zb = None
    chunks = []
    for c in range(CONF_WIDTH // LANES):
        lanes = slice(c * LANES, (c + 1) * LANES)
        yield from _causal_conv_stages(c_ext, p.dww, CONF_CONV, CONF_HALO, lanes, res, p.dwb)
        chunks.append(res["conv"])
        if c == 0:
            za_ref[...] = _silu(proj(C_ZA, GDN_WIDTH))
        if c == 2:
            zb = _silu(proj(C_ZB, CONF_WIDTH))
        yield
    acc = jnp.concatenate(chunks, axis=1)
    c_ext[0:CONF_HALO, :] = c_ext[ROWS:ROWS + CONF_HALO, :]
    mu = jnp.mean(acc, axis=-1, keepdims=True)
    cen = acc - mu
    var = jnp.mean(cen * cen, axis=-1, keepdims=True)
    ln = cen * lax.rsqrt(var + NORM_EPS) * p.lnw[...] + p.lnb[...]
    put_ob(_silu(ln) * zb)


def _gdn_out(e, p, res, za_ref, put_oa):
    for hd in range(GDN_HEADS):
        put_oa(hd, _rms(res["o"][hd], p.onorm[...]) * za_ref[:, e.hsl[hd]])


def _dup_kv(kv):
    lo = lax.broadcasted_iota(jnp.int32, (ROWS, LANES), 1) < SWA_HEAD_DIM
    kk, vv = kv[:, :LANES], kv[:, LANES:]
    kr = pltpu.roll(kk, SWA_HEAD_DIM, axis=1)
    vr = pltpu.roll(vv, SWA_HEAD_DIM, axis=1)
    k2 = (jnp.where(lo, kk, kr), jnp.where(lo, kr, kk))
    v2 = (jnp.where(lo, vv, vr), jnp.where(lo, vr, vv))
    return k2, v2


def _attention_stages(get_q, has_prev, sinks_ref, k2_ref, v2_ref, get_z, put):
    quad = 4
    lo_h = lax.broadcasted_iota(jnp.int32, (HALF, LANES), 1) < SWA_HEAD_DIM
    groups = []
    for u in range(ROWS // HALF):
        base = 0 if u == 0 else KV_PREV + HALF
        r = (lax.broadcasted_iota(jnp.int32, (quad * HALF, N_KEYS), 0) & (HALF - 1)) + u * HALF
        c = lax.broadcasted_iota(jnp.int32, (quad * HALF, N_KEYS), 1)
        if u == 0:
            is_meta, kp = c < N_META, c - N_META - ROWS
        else:
            is_meta, kp = c >= N_WIN, c - HALF
        valid = is_meta | ((kp <= r) & (kp > r - SWA_WINDOW) & ((kp >= 0) | has_prev))
        groups.append((slice(u * HALF, (u + 1) * HALF), base, valid))
    grp = lax.broadcasted_iota(jnp.int32, (quad * HALF, 1), 0) // HALF
    for t in range(SWA_Q_HEADS // quad):
        g = (quad * t) // (SWA_Q_HEADS // SWA_KV_HEADS)
        sink = LOG2_E * jnp.where(grp == 0, sinks_ref[quad * t],
                                  jnp.where(grp == 1, sinks_ref[quad * t + 1],
                                            jnp.where(grp == 2, sinks_ref[quad * t + 2],
                                                      sinks_ref[quad * t + 3])))
        out_a, out_b = [], []
        qq = get_q(t)
        for rows, base, valid in groups:
            qa, qb = qq[rows, :LANES], qq[rows, LANES:]
            lhs = jnp.concatenate([jnp.where(lo_h, qa, 0.0), jnp.where(lo_h, 0.0, qa),
                                   jnp.where(lo_h, qb, 0.0), jnp.where(lo_h, 0.0, qb)], axis=0)
            s = lax.dot_general(lhs.astype(jnp.bfloat16), k2_ref[g, base:base + N_KEYS, :], _NT,
                                preferred_element_type=jnp.float32)
            s = jnp.where(valid, s, -jnp.inf)
            m = jnp.maximum(jnp.max(s, axis=-1, keepdims=True), sink)
            ex = jnp.exp2(s - m)
            den = jnp.sum(ex, axis=-1, keepdims=True) + jnp.exp2(sink - m)
            o4 = jnp.dot(ex.astype(jnp.bfloat16), v2_ref[g, base:base + N_KEYS, :],
                         preferred_element_type=jnp.float32) / den
            out_a.append(jnp.where(lo_h, o4[0:HALF], o4[HALF:2 * HALF]))
            out_b.append(jnp.where(lo_h, o4[2 * HALF:3 * HALF], o4[3 * HALF:]))
            yield
        zz = get_z(t)
        put(2 * t, jnp.concatenate(out_a, axis=0) * zz[:, :LANES])
        put(2 * t + 1, jnp.concatenate(out_b, axis=0) * zz[:, LANES:])


def _even_params(pre, win, qkvw, alog, dtb, onorm, dww, dwb, lnw, lnb, wout, post):
    return types.SimpleNamespace(pre=pre, win=win, qkvw=qkvw, alog=alog, dtb=dtb, onorm=onorm,
                                 dww=dww, dwb=dwb, lnw=lnw, lnb=lnb, wout=wout, post=post)


N_EVEN_PARAMS = 12


def _prologue_kernel(meta_ref, *refs):
    p = _even_params(*refs[:N_EVEN_PARAMS])
    (pre1, win1, s0_out, qkvt_out, ct_out, kmeta_out, vmeta_out,
     qkv_ext, c_ext, za_s, mixed_s) = refs[N_EVEN_PARAMS:]
    qkv_ext[0:QKV_HALO, :] = jnp.zeros((QKV_HALO, 3 * GDN_WIDTH), jnp.float32)
    c_ext[0:CONF_HALO, :] = jnp.zeros((CONF_HALO, CONF_WIDTH), jnp.float32)
    s0_out[...] = jnp.zeros(s0_out.shape, jnp.float32)

    h = meta_ref[...]
    proj = functools.partial(_RowProj([_rms(h, p.pre[...]).astype(jnp.bfloat16)], p.win), 0)

    def put_ob(val):
        mixed_s[:, GDN_WIDTH:] = val.astype(jnp.bfloat16)

    def put_oa(hd, val):
        mixed_s[:, hd * GDN_HEAD_DIM:(hd + 1) * GDN_HEAD_DIM] = val.astype(jnp.bfloat16)

    e = _even_front(proj, PAD_ROWS, p, qkv_ext)
    res = {}
    _interleave(_gdn_stages(e, s0_out, res), _conformer_stages(proj, p, c_ext, za_s, put_ob))
    _gdn_out(e, p, res, za_s, put_oa)
    y0 = jnp.dot(mixed_s[...], p.wout[...], preferred_element_type=jnp.float32)
    h1 = h + _rms(y0, p.post[...])
    qkvt_out[...] = qkv_ext[0:QKV_HALO, :]
    ct_out[...] = c_ext[0:CONF_HALO, :]

    hn1 = _rms(h1, pre1[...]).astype(jnp.bfloat16)
    k2, v2 = _dup_kv(jnp.dot(hn1, win1[:, SWA_Q_WIDTH:SWA_Q_WIDTH + 2 * SWA_KV_WIDTH],
                             preferred_element_type=jnp.float32))
    for g in range(SWA_KV_HEADS):
        kmeta_out[g] = k2[g][PAD_ROWS:, :].astype(jnp.bfloat16)
        vmeta_out[g] = v2[g][PAD_ROWS:, :].astype(jnp.bfloat16)


def _main_kernel(nblk, sinks_ref, x_ref, *refs):
    p = _even_params(*refs[:N_EVEN_PARAMS])
    (s0_ref, qkvt_ref, ct_ref, pre1, win1, wout1, post1, kmeta_ref, vmeta_ref,
     out_ref, qkv_ext, c_ext, s_ref, h1_s, k2_ref, v2_ref,
     za_s, mixed_s, gated_s) = refs[N_EVEN_PARAMS:]
    j = pl.program_id(1)
    seqs = range(SEQS)

    @pl.when(j == 0)
    def _():
        for s in seqs:
            qkv_ext[s, 0:QKV_HALO, :] = qkvt_ref[...]
            c_ext[s, 0:CONF_HALO, :] = ct_ref[...]
            s_ref[s] = s0_ref[...]
            for g in range(SWA_KV_HEADS):
                for buf, meta in ((k2_ref, kmeta_ref), (v2_ref, vmeta_ref)):
                    buf[s, g, 0:N_META, :] = meta[g]
                    buf[s, g, KV_TAIL:KV_ROWS, :] = meta[g]
                    buf[s, g, KV_PREV:KV_CUR, :] = jnp.zeros((ROWS, LANES), jnp.bfloat16)

    def put_ob(s, val):
        mixed_s[s * ROWS:(s + 1) * ROWS, GDN_WIDTH:] = val.astype(jnp.bfloat16)

    def put_oa(s, hd, val):
        mixed_s[s * ROWS:(s + 1) * ROWS, hd * GDN_HEAD_DIM:(hd + 1) * GDN_HEAD_DIM] = val.astype(jnp.bfloat16)

    def put_gated(s, pair, val):
        gated_s[s * ROWS:(s + 1) * ROWS, pair * LANES:(pair + 1) * LANES] = val.astype(jnp.bfloat16)

    def step(do_even, do_odd):
        stages = []
        if do_even:
            h = [x_ref[s] for s in seqs]
            proj0 = _RowProj([_rms(h[s], p.pre[...]).astype(jnp.bfloat16) for s in seqs], p.win)
            conf = [_conformer_stages(functools.partial(proj0, s), p, c_ext.at[s], za_s.at[s],
                                      functools.partial(put_ob, s)) for s in seqs]
            e = [_even_front(functools.partial(proj0, s), 0, p, qkv_ext.at[s]) for s in seqs]
            res = [{} for _ in seqs]

            def gdn_stages(s):
                yield from _gdn_stages(e[s], s_ref.at[s], res[s])
                _gdn_out(e[s], p, res[s], za_s.at[s], functools.partial(put_oa, s))

            stages += [gdn_stages(s) for s in seqs]
        if do_odd:
            h1 = [h1_s[s] for s in seqs]
            proj1 = _RowProj([_rms(h1[s], pre1[...]).astype(jnp.bfloat16) for s in seqs], win1)
            for s in seqs:
                k2, v2 = _dup_kv(proj1(s, SWA_Q_WIDTH, 2 * SWA_KV_WIDTH))
                for g in range(SWA_KV_HEADS):
                    k2_ref[s, g, KV_CUR:KV_TAIL, :] = k2[g].astype(jnp.bfloat16)
                    v2_ref[s, g, KV_CUR:KV_TAIL, :] = v2[g].astype(jnp.bfloat16)
            half_w = SWA_Q_WIDTH // 2

            def get_q(s, t):
                part = proj1(s, (t // 2) * half_w, half_w)
                return part[:, (t % 2) * 2 * LANES:(t % 2 + 1) * 2 * LANES] * (SWA_HEAD_DIM ** -0.5 * LOG2_E)

            def get_z(s, t):
                part = proj1(s, SWA_Q_WIDTH + 2 * SWA_KV_WIDTH + (t // 2) * half_w, half_w)
                return _silu(part[:, (t % 2) * 2 * LANES:(t % 2 + 1) * 2 * LANES])

            stages += [_attention_stages(functools.partial(get_q, s), j >= 2, sinks_ref,
                                         k2_ref.at[s], v2_ref.at[s], functools.partial(get_z, s),
                                         functools.partial(put_gated, s)) for s in seqs]
        if do_even:
            stages += conf
        _interleave(*stages)

        if do_odd:
            for s in seqs:
                for g in range(SWA_KV_HEADS):
                    k2_ref[s, g, KV_PREV:KV_CUR, :] = k2_ref[s, g, KV_CUR:KV_TAIL, :]
                    v2_ref[s, g, KV_PREV:KV_CUR, :] = v2_ref[s, g, KV_CUR:KV_TAIL, :]
            y1 = jnp.dot(gated_s[...], wout1[...], preferred_element_type=jnp.float32)
            for s in seqs:
                out_ref[s] = h1[s] + _rms(y1[s * ROWS:(s + 1) * ROWS], post1[...])
        if do_even:
            y0 = jnp.dot(mixed_s[...], p.wout[...], preferred_element_type=jnp.float32)
            for s in seqs:
                h1_s[s] = h[s] + _rms(y0[s * ROWS:(s + 1) * ROWS], p.post[...])

    pl.when(j == 0)(functools.partial(step, True, False))
    pl.when((j > 0) & (j < nblk))(functools.partial(step, True, True))
    pl.when(j == nblk)(functools.partial(step, False, True))


def _lane_pad(v, offset):
    return jnp.zeros((1, LANES), jnp.float32).at[0, offset:offset + v.shape[0]].set(v)


def kernel(x, meta_tokens, even_pre_norm, even_w_in, even_qkv_conv, even_a_log, even_dt_bias,
           even_out_norm, even_dw_conv, even_dw_bias, even_ln_w, even_ln_b, even_w_out,
           even_post_norm, odd_pre_norm, odd_w_in, odd_sinks, odd_w_out, odd_post_norm):
    bsz, seq, d = x.shape
    assert d == D_MODEL and seq % ROWS == 0 and bsz % SEQS == 0
    nblk = seq // ROWS
    f32, bf16 = jnp.float32, jnp.bfloat16

    w0 = even_w_in[0].astype(bf16)
    gates0 = C_ZA + GDN_WIDTH
    w_in0 = jnp.concatenate(
        [w0[:, :gates0], w0[:, gates0 + 2 * GDN_HEADS:], w0[:, gates0:gates0 + 2 * GDN_HEADS],
         jnp.zeros((d, LANES - 2 * GDN_HEADS), bf16)], axis=1)
    w_out0 = even_w_out[0].astype(bf16)
    w_in1 = odd_w_in[0].astype(bf16)
    w_out1 = odd_w_out[0].astype(bf16)
    meta_blk = jnp.concatenate([jnp.zeros((PAD_ROWS, d), f32), meta_tokens.astype(f32)], axis=0)
    row = lambda v: v.reshape(1, -1).astype(f32)
    even_args = (row(even_pre_norm[0]), w_in0, even_qkv_conv[0].astype(f32),
                 _lane_pad(even_a_log[0], GDN_HEADS), _lane_pad(even_dt_bias[0], GDN_HEADS),
                 row(even_out_norm[0]), even_dw_conv[0].astype(f32), row(even_dw_bias[0]),
                 row(even_ln_w[0]), row(even_ln_b[0]), w_out0, row(even_post_norm[0]))
    even_shapes = [a.shape for a in even_args]

    state_shapes = [
        jax.ShapeDtypeStruct((GDN_HEADS, GDN_HEAD_DIM, GDN_HEAD_DIM), f32),
        jax.ShapeDtypeStruct((QKV_HALO, 3 * GDN_WIDTH), f32),
        jax.ShapeDtypeStruct((CONF_HALO, CONF_WIDTH), f32),
        jax.ShapeDtypeStruct((SWA_KV_HEADS, N_META, LANES), bf16),
        jax.ShapeDtypeStruct((SWA_KV_HEADS, N_META, LANES), bf16),
    ]

    once = pl.Buffered(1)

    def full1(shape):
        return pl.BlockSpec(shape, lambda i: (0,) * len(shape), pipeline_mode=once)

    s0, qkv_tail, c_tail, k_meta, v_meta = pl.pallas_call(
        _prologue_kernel,
        grid=(1,),
        in_specs=[full1((ROWS, d))] + [full1(s) for s in even_shapes]
        + [full1((1, d)), full1((d, ODD_COLS))],
        out_specs=[pl.BlockSpec(s.shape, lambda i, n=len(s.shape): (0,) * n) for s in state_shapes],
        out_shape=state_shapes,
        scratch_shapes=[pltpu.VMEM((QKV_HALO + ROWS, 3 * GDN_WIDTH), f32),
                        pltpu.VMEM((CONF_HALO + ROWS, CONF_WIDTH), f32),
                        pltpu.VMEM((ROWS, GDN_WIDTH), f32),
                        pltpu.VMEM((ROWS, d), bf16)],
        compiler_params=pltpu.CompilerParams(dimension_semantics=("arbitrary",),
                                             vmem_limit_bytes=VMEM_LIMIT_BYTES),
        name="meta_block",
    )(meta_blk, *even_args, row(odd_pre_norm[0]), w_in1)

    def full2(shape):
        return pl.BlockSpec(shape, lambda b, j, s: (0,) * len(shape), pipeline_mode=once)

    out = pl.pallas_call(
        functools.partial(_main_kernel, nblk),
        grid_spec=pltpu.PrefetchScalarGridSpec(
            num_scalar_prefetch=1,
            grid=(bsz // SEQS, nblk + 1),
            in_specs=[pl.BlockSpec((SEQS, ROWS, d), lambda b, j, s: (b, jnp.minimum(j, nblk - 1), 0))]
            + [full2(s) for s in even_shapes]
            + [full2(s.shape) for s in state_shapes[:3]]
            + [full2((1, d)), full2((d, ODD_COLS)), full2((SWA_Q_WIDTH, d)), full2((1, d))]
            + [full2(s.shape) for s in state_shapes[3:]],
            out_specs=pl.BlockSpec((SEQS, ROWS, d), lambda b, j, s: (b, jnp.maximum(j - 1, 0), 0)),
            scratch_shapes=[
                pltpu.VMEM((SEQS, QKV_HALO + ROWS, 3 * GDN_WIDTH), f32),
                pltpu.VMEM((SEQS, CONF_HALO + ROWS, CONF_WIDTH), f32),
                pltpu.VMEM((SEQS, GDN_HEADS, GDN_HEAD_DIM, GDN_HEAD_DIM), f32),
                pltpu.VMEM((SEQS, ROWS, d), f32),
                pltpu.VMEM((SEQS, SWA_KV_HEADS, KV_ROWS, LANES), bf16),
                pltpu.VMEM((SEQS, SWA_KV_HEADS, KV_ROWS, LANES), bf16),
                pltpu.VMEM((SEQS, ROWS, GDN_WIDTH), f32),
                pltpu.VMEM((SEQS * ROWS, d), bf16),
                pltpu.VMEM((SEQS * ROWS, d), bf16),
            ],
        ),
        out_shape=jax.ShapeDtypeStruct((bsz, seq, d), f32),
        compiler_params=pltpu.CompilerParams(dimension_semantics=("parallel", "arbitrary"),
                                             vmem_limit_bytes=VMEM_LIMIT_BYTES),
        name="trunk",
    )(odd_sinks[0].astype(f32), x, *even_args, s0, qkv_tail, c_tail,
      row(odd_pre_norm[0]), w_in1, w_out1, row(odd_post_norm[0]), k_meta, v_meta)
    return out
```

```python
import functools
import types

import jax
import jax.numpy as jnp
from jax import lax
from jax.experimental import pallas as pl
from jax.experimental.pallas import tpu as pltpu

D_MODEL = 1024
N_META = 16
NORM_EPS = 1e-6
LOG2_E = 1.4426950408889634
NEG_LOG2_E = -LOG2_E

GDN_HEADS = 4
GDN_HEAD_DIM = 128
GDN_WIDTH = GDN_HEADS * GDN_HEAD_DIM
GDN_CONV = 4
GDN_CHUNK = 64
CONF_WIDTH = D_MODEL - GDN_WIDTH
CONF_CONV = 31

SWA_HEAD_DIM = 64
SWA_Q_HEADS = D_MODEL // SWA_HEAD_DIM
SWA_KV_HEADS = 2
SWA_WINDOW = 128
SWA_Q_WIDTH = SWA_Q_HEADS * SWA_HEAD_DIM
SWA_KV_WIDTH = SWA_KV_HEADS * SWA_HEAD_DIM
ODD_COLS = 2 * SWA_Q_WIDTH + 2 * SWA_KV_WIDTH

LANES = 128
SUBLANES = 8
ROWS = 128
HALF = ROWS // 2
SEQS = 2
PAD_ROWS = ROWS - N_META

C_QKV = 0
C_ZA = C_QKV + 3 * GDN_WIDTH
C_GV = C_ZA + GDN_WIDTH
C_GG = C_GV + CONF_WIDTH
C_ZB = C_GG + CONF_WIDTH
C_BA = C_ZB + CONF_WIDTH
EVEN_COLS = C_BA + LANES

QKV_HALO = 8
CONF_HALO = 32

KV_PREV, KV_CUR, KV_TAIL = N_META, N_META + ROWS, N_META + 2 * ROWS
KV_ROWS = KV_TAIL + N_META
N_WIN = 3 * HALF
N_KEYS = N_META + N_WIN

VMEM_LIMIT_BYTES = 56 * 1024 * 1024

_NT = (((1,), (1,)), ((), ()))


def _dot(a, b):
    return jnp.dot(a.astype(jnp.bfloat16), b.astype(jnp.bfloat16), preferred_element_type=jnp.float32)


def _dot_nt(a, b):
    return lax.dot_general(a.astype(jnp.bfloat16), b.astype(jnp.bfloat16), _NT,
                           preferred_element_type=jnp.float32)


def _rms(x, w):
    return x * lax.rsqrt(jnp.mean(x * x, axis=-1, keepdims=True) + NORM_EPS) * w


def _sigmoid(x):
    return 1.0 / (1.0 + jnp.exp2(x * NEG_LOG2_E))


def _silu(x):
    return x * _sigmoid(x)


def _softplus(x):
    return jnp.maximum(x, 0.0) + jnp.log(1.0 + jnp.exp(-jnp.abs(x)))


def _l2n(x):
    return x * lax.rsqrt(jnp.sum(x * x, axis=-1, keepdims=True) + NORM_EPS)


def _interleave(*stages):
    live = list(stages)
    while live:
        for g in list(live):
            try:
                next(g)
            except StopIteration:
                live.remove(g)


class _RowProj:
    def __init__(self, blocks, *w_refs):
        self.lhs = blocks[0] if len(blocks) == 1 else jnp.concatenate(blocks, axis=0)
        self.w_refs = w_refs
        self.cache = {}

    def __call__(self, s, c0, width):
        if (c0, width) not in self.cache:
            start = 0
            for w_ref in self.w_refs:
                if c0 < start + w_ref.shape[1]:
                    break
                start += w_ref.shape[1]
            self.cache[(c0, width)] = jnp.dot(self.lhs, w_ref[:, c0 - start:c0 - start + width],
                                              preferred_element_type=jnp.float32)
        return self.cache[(c0, width)][s * ROWS:(s + 1) * ROWS]


def _gdn_stages(e, s_ref, res):
    n = ROWS
    q, k, v, beta, gc, glast, sd = e.q, e.k, e.v, e.beta, e.gc, e.glast, e.sd
    heads = range(len(q))
    row = lax.broadcasted_iota(jnp.int32, (n, n), 0)
    col = lax.broadcasted_iota(jnp.int32, (n, n), 1)
    same = (row >= GDN_CHUNK) == (col >= GDN_CHUNK)
    incl = same & (col <= row)
    strict = same & (col < row)
    eye = row == col
    ident = jnp.where(eye, 1.0, 0.0)

    kb = [k[h] * beta[h] for h in heads]
    kq = [_dot_nt(jnp.concatenate([kb[h], q[h]], axis=0), k[h]) for h in heads]
    yield

    gc_b = [jnp.broadcast_to(gc[h], (n, n)) for h in heads]
    gc_row = [jnp.sum(jnp.where(eye, gc_b[h], 0.0), axis=0, keepdims=True) for h in heads]
    decay = [jnp.exp(jnp.where(incl, gc_b[h] - gc_row[h], 0.0)) for h in heads]
    a = [kq[h][:n] * jnp.where(strict, decay[h], 0.0) for h in heads]
    qk = [kq[h][n:] * jnp.where(incl, decay[h], 0.0) for h in heads]

    p = [ident - a[h] for h in heads]
    bm = [_dot(a[h], a[h]) for h in heads]
    yield
    levels = GDN_CHUNK.bit_length() - 1
    for lvl in range(1, levels):
        if lvl < levels - 1:
            r = [_dot(bm[h], jnp.concatenate([bm[h], p[h]], axis=1)) for h in heads]
            bm = [r[h][:, :n] for h in heads]
            p = [p[h] + r[h][:, n:] for h in heads]
        else:
            p = [p[h] + _dot(bm[h], p[h]) for h in heads]
        yield

    egc = [jnp.exp(gc[h]) for h in heads]
    uw = [_dot(p[h], jnp.concatenate([v[h] * beta[h], kb[h] * egc[h]], axis=1)) for h in heads]
    qd = [q[h] * egc[h] for h in heads]
    kd_t = [jnp.transpose(k[h] * jnp.exp(glast[h] - gc[h])) for h in heads]
    yield

    outs = [[] for _ in heads]
    state = [s_ref[h] for h in heads]
    zeros = jnp.zeros((GDN_CHUNK, GDN_HEAD_DIM), jnp.float32)
    for c in range(n // GDN_CHUNK):
        sl = slice(c * GDN_CHUNK, (c + 1) * GDN_CHUNK)
        r1 = [_dot(jnp.concatenate([uw[h][sl, GDN_HEAD_DIM:], qd[h][sl]], axis=0), state[h])
              for h in heads]
        yield
        v_new = [uw[h][sl, :GDN_HEAD_DIM] - r1[h][:GDN_CHUNK] for h in heads]
        v_full = [jnp.concatenate([v_new[h], zeros] if c == 0 else [zeros, v_new[h]], axis=0)
                  for h in heads]
        r2 = [_dot(jnp.concatenate([qk[h][sl], kd_t[h]], axis=0), v_full[h]) for h in heads]
        for h in heads:
            outs[h].append(r1[h][GDN_CHUNK:] + r2[h][:GDN_CHUNK])
            state[h] = state[h] * sd[h][c] + r2[h][GDN_CHUNK:]
        yield
    for h in heads:
        s_ref[h] = state[h]
    res["o"] = [jnp.concatenate(o, axis=0) for o in outs]


def _shift_rows_up(x, k):
    n = x.shape[0] // SUBLANES
    keep = lax.broadcasted_iota(jnp.int32, (SUBLANES, x.shape[1]), 0) < SUBLANES - k
    rot = [pltpu.roll(x[i * SUBLANES:(i + 1) * SUBLANES], SUBLANES - k, axis=0) for i in range(n)]
    return jnp.concatenate([jnp.where(keep, rot[i], rot[(i + 1) % n]) for i in range(n)], axis=0)


def _causal_conv(ext_ref, w_ref, taps, halo, lanes, bias_ref=None):
    ext = ext_ref[:, lanes]
    first = halo - (taps - 1)
    acc = None if bias_ref is None else bias_ref[:, lanes]
    shifted, shifted_by = ext, 0
    for phase in range(SUBLANES):
        starts = [first + t for t in range(taps) if (first + t) % SUBLANES == phase]
        if not starts:
            continue
        if phase:
            shifted, shifted_by = _shift_rows_up(shifted, phase - shifted_by), phase
        for st in starts:
            t = st - first
            term = w_ref[t:t + 1, lanes] * shifted[st - phase:st - phase + ROWS, :]
            acc = term if acc is None else acc + term
    return acc


def _even_front(proj, pad_rows, p, qkv_ext):
    e = types.SimpleNamespace()
    qkv_ext[QKV_HALO:QKV_HALO + ROWS, :] = proj(C_QKV, 3 * GDN_WIDTH)
    qkv = [None] * (3 * GDN_HEADS)
    for c in [part * GDN_HEADS + hd for hd in range(GDN_HEADS) for part in range(3)]:
        lanes = slice(c * LANES, (c + 1) * LANES)
        qkv[c] = _silu(_causal_conv(qkv_ext, p.qkvw, GDN_CONV, QKV_HALO, lanes))
    qkv_ext[0:QKV_HALO, :] = qkv_ext[ROWS:ROWS + QKV_HALO, :]

    gate = proj(C_BA, LANES)
    rowi = lax.broadcasted_iota(jnp.int32, (ROWS, LANES), 0)
    beta_all = _sigmoid(gate)
    g_all = -jnp.exp(p.alog[...]) * _softplus(gate + p.dtb[...])
    if pad_rows:
        g_all = jnp.where(rowi >= pad_rows, g_all, 0.0)
    r2 = lax.broadcasted_iota(jnp.int32, (ROWS, ROWS), 0)
    c2 = lax.broadcasted_iota(jnp.int32, (ROWS, ROWS), 1)
    tri = jnp.where(((r2 >= GDN_CHUNK) == (c2 >= GDN_CHUNK)) & (c2 <= r2), 1.0, 0.0)
    gc_all = jnp.dot(tri, g_all, preferred_element_type=jnp.float32,
                     precision=lax.Precision.HIGHEST)

    top = rowi[:, 0:1] < GDN_CHUNK
    hs = range(GDN_HEADS)
    e.hsl = [slice(hd * GDN_HEAD_DIM, (hd + 1) * GDN_HEAD_DIM) for hd in hs]
    e.q = [_l2n(qkv[hd]) * (GDN_HEAD_DIM ** -0.5) for hd in hs]
    e.k = [_l2n(qkv[GDN_HEADS + hd]) for hd in hs]
    e.v = [qkv[2 * GDN_HEADS + hd] for hd in hs]
    e.beta = [beta_all[:, hd:hd + 1] for hd in hs]
    e.gc = [gc_all[:, GDN_HEADS + hd:GDN_HEADS + hd + 1] for hd in hs]
    gl0 = [e.gc[hd][GDN_CHUNK - 1:GDN_CHUNK, :] for hd in hs]
    gl1 = [e.gc[hd][ROWS - 1:ROWS, :] for hd in hs]
    e.glast = [jnp.where(top, gl0[hd], gl1[hd]) for hd in hs]
    e.sd = [(jnp.exp(gl0[hd]), jnp.exp(gl1[hd])) for hd in hs]
    return e


def _conformer_stages(proj, p, c_ext, za_ref, put_ob):
    c_ext[CONF_HALO:CONF_HALO + ROWS, :] = proj(C_GV, CONF_WIDTH) * _sigmoid(proj(C_GG, CONF_WIDTH))
    yield
    zb = None
    chunks = []
    for c in range(CONF_WIDTH // LANES):
        lanes = slice(c * LANES, (c + 1) * LANES)
        chunks.append(_causal_conv(c_ext, p.dww, CONF_CONV, CONF_HALO, lanes, p.dwb))
        if c == 0:
            za_ref[...] = _silu(proj(C_ZA, GDN_WIDTH))
        if c == 2:
            zb = _silu(proj(C_ZB, CONF_WIDTH))
        yield
    acc = jnp.concatenate(chunks, axis=1)
    c_ext[0:CONF_HALO, :] = c_ext[ROWS:ROWS + CONF_HALO, :]
    mu = jnp.mean(acc, axis=-1, keepdims=True)
    cen = acc - mu
    var = jnp.mean(cen * cen, axis=-1, keepdims=True)
    ln = cen * lax.rsqrt(var + NORM_EPS) * p.lnw[...] + p.lnb[...]
    put_ob(_silu(ln) * zb)


def _gdn_out(e, p, res, za_ref, put_oa):
    for hd in range(GDN_HEADS):
        put_oa(hd, _rms(res["o"][hd], p.onorm[...]) * za_ref[:, e.hsl[hd]])


def _dup_kv(kv):
    lo = lax.broadcasted_iota(jnp.int32, (ROWS, LANES), 1) < SWA_HEAD_DIM
    kk, vv = kv[:, :LANES], kv[:, LANES:]
    kr = pltpu.roll(kk, SWA_HEAD_DIM, axis=1)
    vr = pltpu.roll(vv, SWA_HEAD_DIM, axis=1)
    k2 = (jnp.where(lo, kk, kr), jnp.where(lo, kr, kk))
    v2 = (jnp.where(lo, vv, vr), jnp.where(lo, vr, vv))
    return k2, v2


def _attention_stages(get_q, has_prev, sinks_ref, k2_ref, v2_ref, get_z, put):
    quad = 4
    lo_h = lax.broadcasted_iota(jnp.int32, (HALF, LANES), 1) < SWA_HEAD_DIM
    groups = []
    for u in range(ROWS // HALF):
        base = 0 if u == 0 else KV_PREV + HALF
        r = (lax.broadcasted_iota(jnp.int32, (quad * HALF, N_KEYS), 0) & (HALF - 1)) + u * HALF
        c = lax.broadcasted_iota(jnp.int32, (quad * HALF, N_KEYS), 1)
        if u == 0:
            is_meta, kp = c < N_META, c - N_META - ROWS
        else:
            is_meta, kp = c >= N_WIN, c - HALF
        valid = is_meta | ((kp <= r) & (kp > r - SWA_WINDOW) & ((kp >= 0) | has_prev))
        groups.append((slice(u * HALF, (u + 1) * HALF), base, valid))
    grp = lax.broadcasted_iota(jnp.int32, (quad * HALF, 1), 0) // HALF
    for t in range(SWA_Q_HEADS // quad):
        g = (quad * t) // (SWA_Q_HEADS // SWA_KV_HEADS)
        sink = LOG2_E * jnp.where(grp == 0, sinks_ref[quad * t],
                                  jnp.where(grp == 1, sinks_ref[quad * t + 1],
                                            jnp.where(grp == 2, sinks_ref[quad * t + 2],
                                                      sinks_ref[quad * t + 3])))
        out_a, out_b = [], []
        qq = get_q(t)
        for rows, base, valid in groups:
            qa, qb = qq[rows, :LANES], qq[rows, LANES:]
            lhs = jnp.concatenate([jnp.where(lo_h, qa, 0.0), jnp.where(lo_h, 0.0, qa),
                                   jnp.where(lo_h, qb, 0.0), jnp.where(lo_h, 0.0, qb)], axis=0)
            s = lax.dot_general(lhs.astype(jnp.bfloat16), k2_ref[g, base:base + N_KEYS, :], _NT,
                                preferred_element_type=jnp.float32)
            s = jnp.where(valid, s, -jnp.inf)
            m = jnp.maximum(jnp.max(s, axis=-1, keepdims=True), sink)
            ex = jnp.exp2(s - m)
            den = jnp.sum(ex, axis=-1, keepdims=True) + jnp.exp2(sink - m)
            o4 = jnp.dot(ex.astype(jnp.bfloat16), v2_ref[g, base:base + N_KEYS, :],
                         preferred_element_type=jnp.float32) / den
            out_a.append(jnp.where(lo_h, o4[0:HALF], o4[HALF:2 * HALF]))
            out_b.append(jnp.where(lo_h, o4[2 * HALF:3 * HALF], o4[3 * HALF:]))
        yield
        zz = get_z(t)
        put(2 * t, jnp.concatenate(out_a, axis=0) * zz[:, :LANES])
        put(2 * t + 1, jnp.concatenate(out_b, axis=0) * zz[:, LANES:])


def _even_params(pre, win_a, win_b, qkvw, alog, dtb, onorm, dww, dwb, lnw, lnb, wout, post):
    return types.SimpleNamespace(pre=pre, win=(win_a, win_b), qkvw=qkvw, alog=alog, dtb=dtb,
                                 onorm=onorm, dww=dww, dwb=dwb, lnw=lnw, lnb=lnb, wout=wout, post=post)


N_EVEN_PARAMS = 13


def _prologue_kernel(meta_ref, *refs):
    p = _even_params(*refs[:N_EVEN_PARAMS])
    (pre1, wkv1, s0_out, qkvt_out, ct_out, kmeta_out, vmeta_out,
     qkv_ext, c_ext, za_s, mixed_s) = refs[N_EVEN_PARAMS:]
    qkv_ext[0:QKV_HALO, :] = jnp.zeros((QKV_HALO, 3 * GDN_WIDTH), jnp.float32)
    c_ext[0:CONF_HALO, :] = jnp.zeros((CONF_HALO, CONF_WIDTH), jnp.float32)
    s0_out[...] = jnp.zeros(s0_out.shape, jnp.float32)

    h = meta_ref[...]
    proj = functools.partial(_RowProj([_rms(h, p.pre[...]).astype(jnp.bfloat16)], *p.win), 0)

    def put_ob(val):
        mixed_s[:, GDN_WIDTH:] = val.astype(jnp.bfloat16)

    def put_oa(hd, val):
        mixed_s[:, hd * GDN_HEAD_DIM:(hd + 1) * GDN_HEAD_DIM] = val.astype(jnp.bfloat16)

    e = _even_front(proj, PAD_ROWS, p, qkv_ext)
    res = {}
    _interleave(_gdn_stages(e, s0_out, res), _conformer_stages(proj, p, c_ext, za_s, put_ob))
    _gdn_out(e, p, res, za_s, put_oa)
    y0 = jnp.dot(mixed_s[...], p.wout[...], preferred_element_type=jnp.float32)
    h1 = h + _rms(y0, p.post[...])
    qkvt_out[...] = qkv_ext[0:QKV_HALO, :]
    ct_out[...] = c_ext[0:CONF_HALO, :]

    hn1 = _rms(h1, pre1[...]).astype(jnp.bfloat16)
    k2, v2 = _dup_kv(jnp.dot(hn1, wkv1[...], preferred_element_type=jnp.float32))
    for g in range(SWA_KV_HEADS):
        kmeta_out[g] = k2[g][PAD_ROWS:, :].astype(jnp.bfloat16)
        vmeta_out[g] = v2[g][PAD_ROWS:, :].astype(jnp.bfloat16)


def _main_kernel(nblk, sinks_ref, x_ref, *refs):
    p = _even_params(*refs[:N_EVEN_PARAMS])
    (s0_ref, qkvt_ref, ct_ref, pre1, win1, wout1, post1, kmeta_ref, vmeta_ref,
     out_ref, qkv_ext, c_ext, s_ref, h1_s, k2_ref, v2_ref,
     za_s, mixed_s, gated_s) = refs[N_EVEN_PARAMS:]
    j = pl.program_id(1)
    seqs = range(SEQS)

    @pl.when(j == 0)
    def _():
        for s in seqs:
            qkv_ext[s, 0:QKV_HALO, :] = qkvt_ref[...]
            c_ext[s, 0:CONF_HALO, :] = ct_ref[...]
            s_ref[s] = s0_ref[...]
            for g in range(SWA_KV_HEADS):
                for buf, meta in ((k2_ref, kmeta_ref), (v2_ref, vmeta_ref)):
                    buf[s, g, 0:N_META, :] = meta[g]
                    buf[s, g, KV_TAIL:KV_ROWS, :] = meta[g]
                    buf[s, g, KV_PREV:KV_CUR, :] = jnp.zeros((ROWS, LANES), jnp.bfloat16)

    def put_ob(s, val):
        mixed_s[s * ROWS:(s + 1) * ROWS, GDN_WIDTH:] = val.astype(jnp.bfloat16)

    def put_oa(s, hd, val):
        mixed_s[s * ROWS:(s + 1) * ROWS, hd * GDN_HEAD_DIM:(hd + 1) * GDN_HEAD_DIM] = val.astype(jnp.bfloat16)

    def put_gated(s, pair, val):
        gated_s[s * ROWS:(s + 1) * ROWS, pair * LANES:(pair + 1) * LANES] = val.astype(jnp.bfloat16)

    def step(do_even, do_odd):
        stages = []
        if do_even:
            h = [x_ref[s] for s in seqs]
            proj0 = _RowProj([_rms(h[s], p.pre[...]).astype(jnp.bfloat16) for s in seqs], *p.win)
            conf = [_conformer_stages(functools.partial(proj0, s), p, c_ext.at[s], za_s.at[s],
                                      functools.partial(put_ob, s)) for s in seqs]
            e = [_even_front(functools.partial(proj0, s), 0, p, qkv_ext.at[s]) for s in seqs]
            res = [{} for _ in seqs]

            def gdn_stages(s):
                yield from _gdn_stages(e[s], s_ref.at[s], res[s])
                _gdn_out(e[s], p, res[s], za_s.at[s], functools.partial(put_oa, s))

            stages += conf + [gdn_stages(s) for s in seqs]
        if do_odd:
            h1 = [h1_s[s] for s in seqs]
            proj1 = _RowProj([_rms(h1[s], pre1[...]).astype(jnp.bfloat16) for s in seqs], win1)
            for s in seqs:
                k2, v2 = _dup_kv(proj1(s, SWA_Q_WIDTH, 2 * SWA_KV_WIDTH))
                for g in range(SWA_KV_HEADS):
                    k2_ref[s, g, KV_CUR:KV_TAIL, :] = k2[g].astype(jnp.bfloat16)
                    v2_ref[s, g, KV_CUR:KV_TAIL, :] = v2[g].astype(jnp.bfloat16)
            half_w = SWA_Q_WIDTH // 2

            def get_q(s, t):
                part = proj1(s, (t // 2) * half_w, half_w)
                return part[:, (t % 2) * 2 * LANES:(t % 2 + 1) * 2 * LANES] * (SWA_HEAD_DIM ** -0.5 * LOG2_E)

            def get_z(s, t):
                part = proj1(s, SWA_Q_WIDTH + 2 * SWA_KV_WIDTH + (t // 2) * half_w, half_w)
                return _silu(part[:, (t % 2) * 2 * LANES:(t % 2 + 1) * 2 * LANES])

            stages += [_attention_stages(functools.partial(get_q, s), j >= 2, sinks_ref,
                                         k2_ref.at[s], v2_ref.at[s], functools.partial(get_z, s),
                                         functools.partial(put_gated, s)) for s in seqs]
        _interleave(*stages)

        if do_odd:
            for s in seqs:
                for g in range(SWA_KV_HEADS):
                    k2_ref[s, g, KV_PREV:KV_CUR, :] = k2_ref[s, g, KV_CUR:KV_TAIL, :]
                    v2_ref[s, g, KV_PREV:KV_CUR, :] = v2_ref[s, g, KV_CUR:KV_TAIL, :]
            y1 = jnp.dot(gated_s[...], wout1[...], preferred_element_type=jnp.float32)
            for s in seqs:
                out_ref[s] = h1[s] + _rms(y1[s * ROWS:(s + 1) * ROWS], post1[...])
        if do_even:
            y0 = jnp.dot(mixed_s[...], p.wout[...], preferred_element_type=jnp.float32)
            for s in seqs:
                h1_s[s] = h[s] + _rms(y0[s * ROWS:(s + 1) * ROWS], p.post[...])

    pl.when(j == 0)(functools.partial(step, True, False))
    pl.when((j > 0) & (j < nblk))(functools.partial(step, True, True))
    pl.when(j == nblk)(functools.partial(step, False, True))


def _lane_pad(v, offset):
    return jnp.zeros((1, LANES), jnp.float32).at[0, offset:offset + v.shape[0]].set(v)


def kernel(x, meta_tokens, even_pre_norm, even_w_in, even_qkv_conv, even_a_log, even_dt_bias,
           even_out_norm, even_dw_conv, even_dw_bias, even_ln_w, even_ln_b, even_w_out,
           even_post_norm, odd_pre_norm, odd_w_in, odd_sinks, odd_w_out, odd_post_norm):
    bsz, seq, d = x.shape
    assert d == D_MODEL and seq % ROWS == 0 and bsz % SEQS == 0
    nblk = seq // ROWS
    f32, bf16 = jnp.float32, jnp.bfloat16

    w0 = even_w_in[0].astype(bf16)
    w_in0_b = jnp.concatenate(
        [w0[:, C_GV + 2 * GDN_HEADS:], w0[:, C_GV:C_GV + 2 * GDN_HEADS],
         jnp.zeros((d, LANES - 2 * GDN_HEADS), bf16)], axis=1)
    w_out0 = even_w_out[0].astype(bf16)
    w_in1 = odd_w_in[0].astype(bf16)
    w_out1 = odd_w_out[0].astype(bf16)
    meta_blk = jnp.concatenate([jnp.zeros((PAD_ROWS, d), f32), meta_tokens.astype(f32)], axis=0)
    row = lambda v: v.reshape(1, -1).astype(f32)
    even_args = (row(even_pre_norm[0]), w0, w_in0_b, even_qkv_conv[0].astype(f32),
                 _lane_pad(even_a_log[0], GDN_HEADS), _lane_pad(even_dt_bias[0], GDN_HEADS),
                 row(even_out_norm[0]), even_dw_conv[0].astype(f32), row(even_dw_bias[0]),
                 row(even_ln_w[0]), row(even_ln_b[0]), w_out0, row(even_post_norm[0]))
    even_shapes = [a.shape for a in even_args]
    even_shapes[1] = (d, C_GV)

    state_shapes = [
        jax.ShapeDtypeStruct((GDN_HEADS, GDN_HEAD_DIM, GDN_HEAD_DIM), f32),
        jax.ShapeDtypeStruct((QKV_HALO, 3 * GDN_WIDTH), f32),
        jax.ShapeDtypeStruct((CONF_HALO, CONF_WIDTH), f32),
        jax.ShapeDtypeStruct((SWA_KV_HEADS, N_META, LANES), bf16),
        jax.ShapeDtypeStruct((SWA_KV_HEADS, N_META, LANES), bf16),
    ]

    once = pl.Buffered(1)

    def full1(shape):
        return pl.BlockSpec(shape, lambda i: (0,) * len(shape), pipeline_mode=once)

    s0, qkv_tail, c_tail, k_meta, v_meta = pl.pallas_call(
        _prologue_kernel,
        grid=(1,),
        in_specs=[full1((ROWS, d))] + [full1(s) for s in even_shapes]
        + [full1((1, d)),
           pl.BlockSpec((d, 2 * SWA_KV_WIDTH), lambda i: (0, SWA_Q_WIDTH // (2 * SWA_KV_WIDTH)),
                        pipeline_mode=once)],
        out_specs=[pl.BlockSpec(s.shape, lambda i, n=len(s.shape): (0,) * n) for s in state_shapes],
        out_shape=state_shapes,
        scratch_shapes=[pltpu.VMEM((QKV_HALO + ROWS, 3 * GDN_WIDTH), f32),
                        pltpu.VMEM((CONF_HALO + ROWS, CONF_WIDTH), f32),
                        pltpu.VMEM((ROWS, GDN_WIDTH), f32),
                        pltpu.VMEM((ROWS, d), bf16)],
        compiler_params=pltpu.CompilerParams(dimension_semantics=("arbitrary",),
                                             vmem_limit_bytes=VMEM_LIMIT_BYTES),
        name="meta_block",
    )(meta_blk, *even_args, row(odd_pre_norm[0]), w_in1)

    def full2(shape):
        return pl.BlockSpec(shape, lambda b, j, s: (0,) * len(shape), pipeline_mode=once)

    out = pl.pallas_call(
        functools.partial(_main_kernel, nblk),
        grid_spec=pltpu.PrefetchScalarGridSpec(
            num_scalar_prefetch=1,
            grid=(bsz // SEQS, nblk + 1),
            in_specs=[pl.BlockSpec((SEQS, ROWS, d), lambda b, j, s: (b, jnp.minimum(j, nblk - 1), 0))]
            + [full2(s) for s in even_shapes]
            + [full2(s.shape) for s in state_shapes[:3]]
            + [full2((1, d)), full2((d, ODD_COLS)), full2((SWA_Q_WIDTH, d)), full2((1, d))]
            + [full2(s.shape) for s in state_shapes[3:]],
            out_specs=pl.BlockSpec((SEQS, ROWS, d), lambda b, j, s: (b, jnp.maximum(j - 1, 0), 0)),
            scratch_shapes=[
                pltpu.VMEM((SEQS, QKV_HALO + ROWS, 3 * GDN_WIDTH), f32),
                pltpu.VMEM((SEQS, CONF_HALO + ROWS, CONF_WIDTH), f32),
                pltpu.VMEM((SEQS, GDN_HEADS, GDN_HEAD_DIM, GDN_HEAD_DIM), f32),
                pltpu.VMEM((SEQS, ROWS, d), f32),
                pltpu.VMEM((SEQS, SWA_KV_HEADS, KV_ROWS, LANES), bf16),
                pltpu.VMEM((SEQS, SWA_KV_HEADS, KV_ROWS, LANES), bf16),
                pltpu.VMEM((SEQS, ROWS, GDN_WIDTH), f32),
                pltpu.VMEM((SEQS * ROWS, d), bf16),
                pltpu.VMEM((SEQS * ROWS, d), bf16),
            ],
        ),
        out_shape=jax.ShapeDtypeStruct((bsz, seq, d), f32),
        compiler_params=pltpu.CompilerParams(dimension_semantics=("parallel", "arbitrary"),
                                             vmem_limit_bytes=VMEM_LIMIT_BYTES),
        name="trunk",
    )(odd_sinks[0].astype(f32), x, *even_args, s0, qkv_tail, c_tail,
      row(odd_pre_norm[0]), w_in1, w_out1, row(odd_post_norm[0]), k_meta, v_meta)
    return out
```
